```python
import math
import jax, jax.numpy as jnp
from jax import lax
import numpy as np

D_MODEL = 2048
BATCH = 16
SEQ = 256
DEPTH = 4
DEC_BATCH = 8
DEC_SEQ = 4096
PAST_LEN = 256

GRID_W = 64
D_MIX = D_MODEL
MLA_HEADS = 8
QK_NOPE = 128
QK_ROPE = 64
V_DIM = 128
Q_RANK = D_MODEL // 4
KV_RANK = D_MODEL // 8
ROPE_PAIRS = QK_ROPE // 4
ROPE_THETA = 10000.0
SC_W = D_MIX // 4
HY_W = D_MIX // 4
HY_ORDER = 2
HY_EMB = 33
HY_HIDDEN = 64
PEER_HEADS = 8
N_KEYS = 128
N_EXPERTS = N_KEYS * N_KEYS
PK_HALF = 128
PEER_TOPK = 16
Q_BLOCK = 128
TOK_BLOCK = 128
N_MOD = 6
ALPHA = (2 * DEPTH) ** 0.25
BETA = (8 * DEPTH) ** -0.25
SM_SCALE = (QK_NOPE + QK_ROPE) ** -0.5
LN_EPS = 1e-5
RMS_EPS = 1e-6
IN_SPLITS = (Q_RANK, Q_RANK + KV_RANK, Q_RANK + KV_RANK + QK_ROPE, Q_RANK + KV_RANK + QK_ROPE + 3 * SC_W)
IN_COLS = Q_RANK + KV_RANK + QK_ROPE + 3 * SC_W + 3 * HY_W

kernel_name = 'hybrid_mla_conv_hyena_peer_diffusion_step'


def layer_norm(x, g, b):
    xf = x.astype(jnp.float32)
    mu = jnp.mean(xf, axis=-1, keepdims=True)
    var = jnp.mean(jnp.square(xf - mu), axis=-1, keepdims=True)
    y = (xf - mu) * lax.rsqrt(var + LN_EPS) * g.astype(jnp.float32) + b.astype(jnp.float32)
    return y.astype(x.dtype)


def rms_norm(x, g):
    xf = x.astype(jnp.float32)
    y = xf * lax.rsqrt(jnp.mean(jnp.square(xf), axis=-1, keepdims=True) + RMS_EPS) * g.astype(jnp.float32)
    return y.astype(x.dtype)


def modulation(cond, w, b):
    return (jax.nn.silu(cond) @ w + b).reshape(cond.shape[0], N_MOD, D_MODEL)


def axial_rope_tables(L):
    rows = L // GRID_W
    row = jnp.repeat(jnp.arange(rows), GRID_W)
    col = jnp.tile(jnp.arange(GRID_W), rows)
    pos = jnp.stack([row, col], axis=-1).astype(jnp.float32)
    inv = ROPE_THETA ** (-jnp.arange(ROPE_PAIRS, dtype=jnp.float32) / ROPE_PAIRS)
    ang = pos[:, :, None] * inv
    return jnp.cos(ang), jnp.sin(ang)


def apply_axial_rope(x, cos, sin):
    B, L, H, _ = x.shape
    xr = x.reshape(B, L, H, 2, 2, ROPE_PAIRS)
    x1, x2 = xr[..., 0, :], xr[..., 1, :]
    c = cos[None, :, None].astype(x.dtype)
    s = sin[None, :, None].astype(x.dtype)
    out = jnp.stack([x1 * c - x2 * s, x2 * c + x1 * s], axis=-2)
    return out.reshape(B, L, H, QK_ROPE)


def conv3(x, w):
    xp = jnp.pad(x, ((0, 0), (1, 1), (0, 0)))
    return xp[:, :-2] * w[0] + xp[:, 1:-1] * w[1] + xp[:, 2:] * w[2]


def attention_blocked(q_nope, q_rope, k_nope, k_rope, v):
    B, Lq, H, _ = q_nope.shape
    nblk = Lq // Q_BLOCK
    qn = q_nope.reshape(B, nblk, Q_BLOCK, H, QK_NOPE).swapaxes(0, 1)
    qr = q_rope.reshape(B, nblk, Q_BLOCK, H, QK_ROPE).swapaxes(0, 1)

    def one_block(blk):
        qn_b, qr_b = blk
        s = jnp.einsum('bqhd,bkhd->bhqk', qn_b, k_nope) + jnp.einsum('bqhd,bkd->bhqk', qr_b, k_rope)
        p = jax.nn.softmax(s.astype(jnp.float32) * SM_SCALE, axis=-1).astype(v.dtype)
        return jnp.einsum('bhqk,bkhd->bqhd', p, v)

    o = lax.map(one_block, (qn, qr))
    return o.swapaxes(0, 1).reshape(B, Lq, H * V_DIM)


def hyena_filters(L, w1, b1, w2, b2, w3, freq, decay):
    f32 = jnp.float32
    t = jnp.linspace(0.0, 1.0, L, dtype=f32)[:, None]
    bands = (HY_EMB - 1) // 2
    w = 2.0 * math.pi * jnp.arange(L, dtype=f32) / L
    fr_b = jnp.linspace(1e-4, bands - 1, bands, dtype=f32)
    ang = w[:, None] * fr_b[None]
    z = jnp.concatenate([t, jnp.cos(ang), -jnp.sin(ang)], axis=-1)
    fr = freq.astype(f32)
    hid = jnp.sin(fr * (z @ w1.astype(f32) + b1.astype(f32)))
    hid = jnp.sin(fr * (hid @ w2.astype(f32) + b2.astype(f32)))
    h = (hid @ w3.astype(f32)).reshape(L, 2, HY_ORDER, HY_W)
    h = h * jnp.exp(-t[:, :, None, None] * jnp.abs(decay.astype(f32)))
    k = jnp.concatenate([h[:, 0], jnp.zeros((1, HY_ORDER, HY_W), f32), h[1:, 1][::-1]], axis=0)
    k = k / jnp.sum(jnp.abs(k), axis=0, keepdims=True)
    return jnp.fft.rfft(k, axis=0)


def fft_conv(z, kf):
    L = z.shape[1]
    zf = jnp.fft.rfft(z.astype(jnp.float32), n=2 * L, axis=1)
    y = jnp.fft.irfft(zf * kf[None], n=2 * L, axis=1)[:, :L]
    return y.astype(z.dtype)


def hyena_mixer(u_in, p):
    L = u_in.shape[1]
    u = conv3(u_in, p['hy_sconv_w']) + p['hy_sconv_b']
    v, x1, x2 = jnp.split(u, 3, axis=-1)
    kf = hyena_filters(L, p['hy_f_w1'], p['hy_f_b1'], p['hy_f_w2'], p['hy_f_b2'], p['hy_f_w3'], p['hy_f_freq'], p['hy_decay'])
    z = v
    for o, gate in enumerate((x1, x2)):
        z = gate * (fft_conv(z, kf[:, o]) + p['hy_bias'][o] * z)
    return z


def token_mix(h, p, ctx_ckv, ctx_krope, rope):
    B, L, _ = h.shape
    proj = h @ p['w_in']
    q_lat, kv_lat, k_r, sc_in, hy_in = jnp.split(proj, IN_SPLITS, axis=-1)
    q = (rms_norm(q_lat, p['q_norm']) @ p['w_uq']).reshape(B, L, MLA_HEADS, QK_NOPE + QK_ROPE)
    q_nope, q_rope = q[..., :QK_NOPE], q[..., QK_NOPE:]
    ckv = rms_norm(kv_lat, p['kv_norm'])
    if rope is None:
        ckv_all, kr_all = ckv, k_r
    else:
        cos, sin = rope
        q_rope = apply_axial_rope(q_rope, cos, sin)
        kr_rot = apply_axial_rope(k_r[:, :, None], cos, sin)[:, :, 0]
        ckv_all = jnp.concatenate([ctx_ckv, ckv], axis=1)
        kr_all = jnp.concatenate([ctx_krope, kr_rot], axis=1)
    Lk = ckv_all.shape[1]
    kv = (ckv_all @ p['w_ukv']).reshape(B, Lk, MLA_HEADS, QK_NOPE + V_DIM)
    att = attention_blocked(q_nope, q_rope, kv[..., :QK_NOPE], kr_all, kv[..., QK_NOPE:])
    bg, cg, hs = jnp.split(sc_in, 3, axis=-1)
    sc = bg * conv3(cg * hs, p['sc_conv'])
    hy = hyena_mixer(hy_in, p)
    out = jnp.concatenate([att, sc, hy], axis=-1) @ p['w_out']
    return out, ckv, k_r


def peer(h, wq, pkeys, U, V):
    B, L, D = h.shape
    xt = h.reshape((B * L) // TOK_BLOCK, TOK_BLOCK, D)

    def one_block(xb):
        q = (xb @ wq).reshape(TOK_BLOCK, PEER_HEADS, 2, PK_HALF)
        s = jnp.einsum('thpd,hpnd->thpn', q, pkeys).astype(jnp.float32)
        sv, si = lax.top_k(s, PEER_TOPK)
        comb = (sv[:, :, 0, :, None] + sv[:, :, 1, None, :]).reshape(TOK_BLOCK, PEER_HEADS, PEER_TOPK * PEER_TOPK)
        cidx = (si[:, :, 0, :, None] * N_KEYS + si[:, :, 1, None, :]).reshape(TOK_BLOCK, PEER_HEADS, PEER_TOPK * PEER_TOPK)
        fv, fp = lax.top_k(comb, PEER_TOPK)
        eidx = jnp.take_along_axis(cidx, fp, axis=-1)
        g = jax.nn.softmax(fv, axis=-1)
        a = jax.nn.gelu(jnp.einsum('td,thkd->thk', xb, U[eidx]))
        return jnp.einsum('thk,thkd->td', (g * a).astype(xb.dtype), V[eidx])

    return lax.map(one_block, xt).reshape(B, L, D)


def trunk_layer(x, mod, p, ctx_ckv, ctx_krope, rope):
    shift1, scale1, gate1, shift2, scale2, gate2 = [mod[:, None, i] for i in range(N_MOD)]
    h = x * (1.0 + scale1) + shift1
    mix, ckv, k_r = token_mix(h, p, ctx_ckv, ctx_krope, rope)
    x = layer_norm(ALPHA * x + gate1 * mix, p['ln1_g'], p['ln1_b'])
    h = x * (1.0 + scale2) + shift2
    x = layer_norm(ALPHA * x + gate2 * peer(h, p['pk_query'], p['pk_keys'], p['peer_u'], p['peer_v']), p['ln2_g'], p['ln2_b'])
    return x, ckv, k_r


def setup_inputs(seed: int = 0) -> dict:
    key = jax.random.key(seed)
    keys = iter(jax.random.split(key, 40))

    def nrm(shape, scale):
        return jax.random.normal(next(keys), shape, jnp.float32) * scale

    return {
        'x_prompt': nrm((BATCH, SEQ, D_MODEL), 1.0),
        'x_sample': nrm((DEC_BATCH, DEC_SEQ, D_MODEL), 1.0),
        'cache_ckv': nrm((DEC_BATCH, DEPTH, PAST_LEN, KV_RANK), 1.0),
        'cache_krope': nrm((DEC_BATCH, DEPTH, PAST_LEN, QK_ROPE), 1.0),
        'c': nrm((DEC_BATCH, D_MODEL), 1.0),
        'c_ctx': nrm((D_MODEL,), 1.0),
        'w_mod': nrm((DEPTH, D_MODEL, N_MOD * D_MODEL), 0.5 * D_MODEL ** -0.5),
        'b_mod': nrm((DEPTH, N_MOD * D_MODEL), 0.02),
        'w_in': nrm((DEPTH, D_MODEL, IN_COLS), D_MODEL ** -0.5),
        'mla_q_norm': 1.0 + nrm((DEPTH, Q_RANK), 0.02),
        'mla_kv_norm': 1.0 + nrm((DEPTH, KV_RANK), 0.02),
        'w_uq': nrm((DEPTH, Q_RANK, MLA_HEADS * (QK_NOPE + QK_ROPE)), Q_RANK ** -0.5),
        'w_ukv': nrm((DEPTH, KV_RANK, MLA_HEADS * (QK_NOPE + V_DIM)), KV_RANK ** -0.5),
        'sc_conv': nrm((DEPTH, 3, SC_W), 3 ** -0.5),
        'hy_sconv_w': nrm((DEPTH, 3, 3 * HY_W), 3 ** -0.5),
        'hy_sconv_b': nrm((DEPTH, 3 * HY_W), 0.02),
        'hy_f_w1': nrm((DEPTH, HY_EMB, HY_HIDDEN), HY_EMB ** -0.5),
        'hy_f_b1': nrm((DEPTH, HY_HIDDEN), 0.1),
        'hy_f_w2': nrm((DEPTH, HY_HIDDEN, HY_HIDDEN), HY_HIDDEN ** -0.5),
        'hy_f_b2': nrm((DEPTH, HY_HIDDEN), 0.1),
        'hy_f_w3': nrm((DEPTH, HY_HIDDEN, 2 * HY_ORDER * HY_W), HY_HIDDEN ** -0.5),
        'hy_f_freq': 1.0 + nrm((DEPTH, HY_HIDDEN), 0.02),
        'hy_decay': jax.random.uniform(next(keys), (DEPTH, 2, HY_ORDER, HY_W), jnp.float32, 3.0, 15.0),
        'hy_bias': nrm((DEPTH, HY_ORDER, HY_W), 0.1),
        'w_out': nrm((DEPTH, D_MIX, D_MODEL), BETA * D_MIX ** -0.5),
        'ln1_g': 1.0 + nrm((DEPTH, D_MODEL), 0.02),
        'ln1_b': nrm((DEPTH, D_MODEL), 0.02),
        'ln2_g': 1.0 + nrm((DEPTH, D_MODEL), 0.02),
        'ln2_b': nrm((DEPTH, D_MODEL), 0.02),
        'pk_query': nrm((DEPTH, D_MODEL, PEER_HEADS * 2 * PK_HALF), D_MODEL ** -0.5),
        'pk_keys': nrm((DEPTH, PEER_HEADS, 2, N_KEYS, PK_HALF), PK_HALF ** -0.5),
        'peer_u': nrm((DEPTH, N_EXPERTS, D_MODEL), D_MODEL ** -0.5),
        'peer_v': nrm((DEPTH, N_EXPERTS, D_MODEL), BETA),
    }


def reference(x_prompt, x_sample, cache_ckv, cache_krope, c, c_ctx, w_mod, b_mod, w_in, mla_q_norm, mla_kv_norm,
              w_uq, w_ukv, sc_conv, hy_sconv_w, hy_sconv_b, hy_f_w1, hy_f_b1, hy_f_w2, hy_f_b2, hy_f_w3, hy_f_freq,
              hy_decay, hy_bias, w_out, ln1_g, ln1_b, ln2_g, ln2_b, pk_query, pk_keys, peer_u, peer_v):
    def params_at(l):
        return {
            'w_in': w_in[l], 'q_norm': mla_q_norm[l], 'kv_norm': mla_kv_norm[l], 'w_uq': w_uq[l], 'w_ukv': w_ukv[l],
            'sc_conv': sc_conv[l], 'hy_sconv_w': hy_sconv_w[l], 'hy_sconv_b': hy_sconv_b[l],
            'hy_f_w1': hy_f_w1[l], 'hy_f_b1': hy_f_b1[l], 'hy_f_w2': hy_f_w2[l], 'hy_f_b2': hy_f_b2[l],
            'hy_f_w3': hy_f_w3[l], 'hy_f_freq': hy_f_freq[l], 'hy_decay': hy_decay[l], 'hy_bias': hy_bias[l],
            'w_out': w_out[l], 'ln1_g': ln1_g[l], 'ln1_b': ln1_b[l], 'ln2_g': ln2_g[l], 'ln2_b': ln2_b[l],
            'pk_query': pk_query[l], 'pk_keys': pk_keys[l], 'peer_u': peer_u[l], 'peer_v': peer_v[l],
        }

    xp = x_prompt
    ckv_list, krope_list = [], []
    for l in range(DEPTH):
        mod_ctx = modulation(c_ctx[None], w_mod[l], b_mod[l])
        xp, ckv, k_r = trunk_layer(xp, mod_ctx, params_at(l), None, None, None)
        ckv_list.append(ckv)
        krope_list.append(k_r)
    y_prompt = xp
    new_ckv = jnp.stack(ckv_list, axis=1)
    new_krope = jnp.stack(krope_list, axis=1)

    rope = axial_rope_tables(x_sample.shape[1])
    xs = x_sample
    for l in range(DEPTH):
        mod_lat = modulation(c, w_mod[l], b_mod[l])
        xs, _, _ = trunk_layer(xs, mod_lat, params_at(l), cache_ckv[:, l], cache_krope[:, l], rope)
    y_sample = xs
    return (y_prompt, y_sample, new_ckv, new_krope)
```

```python
import functools
import math

import jax
import jax.numpy as jnp
from jax import lax
from jax.experimental import pallas as pl
from jax.experimental.pallas import tpu as pltpu

F32 = jnp.float32
BF16 = jnp.bfloat16

D_MODEL = 2048
DEPTH = 4
GRID_W = 64
MLA_HEADS = 8
QK_NOPE = 128
QK_ROPE = 64
QK_DIM = QK_NOPE + QK_ROPE
V_DIM = 128
Q_RANK = D_MODEL // 4
KV_RANK = D_MODEL // 8
ROPE_PAIRS = QK_ROPE // 4
ROPE_THETA = 10000.0
SC_W = D_MODEL // 4
HY_W = D_MODEL // 4
HY_ORDER = 2
HY_EMB = 33
HY_HIDDEN = 64
PEER_HEADS = 8
N_KEYS = 128
N_EXPERTS = N_KEYS * N_KEYS
PK_HALF = 128
PEER_TOPK = 16
N_MOD = 6
ALPHA = (2 * DEPTH) ** 0.25
SM_SCALE = QK_DIM ** -0.5
LN_EPS = 1e-5
RMS_EPS = 1e-6

KR_OFF = Q_RANK + KV_RANK
SC_OFF = 1024
HY_OFF = SC_OFF + 3 * SC_W
IN_COLS_PAD = HY_OFF + 3 * HY_W

VMEM_LIMIT = 52 * 1024 * 1024
LANES = 128
SUBLANES = 8


def _cparams(sem):
    return pltpu.CompilerParams(dimension_semantics=sem, vmem_limit_bytes=VMEM_LIMIT)


def _mm_kernel(a_ref, b_ref, o_ref, acc_ref, *, nk):
    part = jnp.dot(a_ref[...].astype(BF16), b_ref[...].astype(BF16), preferred_element_type=F32)
    if nk == 1:
        o_ref[...] = part.astype(o_ref.dtype)
        return
    k = pl.program_id(3)

    @pl.when(k == 0)
    def _():
        acc_ref[...] = part

    @pl.when(k > 0)
    def _():
        acc_ref[...] += part

    @pl.when(k == nk - 1)
    def _():
        o_ref[...] = acc_ref[...].astype(o_ref.dtype)


def _pick(n, pref):
    t = min(n, pref)
    while n % t:
        t //= 2
    return t


def matmul(a, b, *, tm=512, tn=512, tk=2048, out_dtype=F32):
    squeeze = a.ndim == 2 and b.ndim == 2
    if a.ndim == 2:
        a = a[None]
    if b.ndim == 2:
        b = b[None]
    ba, m, k = a.shape
    bb, k2, n = b.shape
    assert k == k2 and (ba == bb or 1 in (ba, bb))
    nb = max(ba, bb)
    tm, tn, tk = _pick(m, tm), _pick(n, tn), _pick(k, tk)
    nk = k // tk
    out = pl.pallas_call(
        functools.partial(_mm_kernel, nk=nk),
        out_shape=jax.ShapeDtypeStruct((nb, m, n), out_dtype),
        grid=(nb, m // tm, n // tn, nk),
        in_specs=[
            pl.BlockSpec((None, tm, tk), lambda g, i, j, kk: (g if ba > 1 else 0, i, kk)),
            pl.BlockSpec((None, tk, tn), lambda g, i, j, kk: (g if bb > 1 else 0, kk, j)),
        ],
        out_specs=pl.BlockSpec((None, tm, tn), lambda g, i, j, kk: (g, i, j)),
        scratch_shapes=[pltpu.VMEM((tm, tn), F32)],
        compiler_params=_cparams(("parallel", "parallel", "parallel", "arbitrary")),
    )(a, b)
    return out[0] if squeeze else out


def _mm_f32_kernel(a_ref, b_ref, o_ref):
    o_ref[...] = jnp.dot(a_ref[...], b_ref[...], preferred_element_type=F32,
                         precision=lax.Precision.HIGHEST)


def matmul_f32(a, b, *, tm=512):
    m, k = a.shape
    n = b.shape[1]
    tm = _pick(m, tm)
    return pl.pallas_call(
        _mm_f32_kernel,
        out_shape=jax.ShapeDtypeStruct((m, n), F32),
        grid=(m // tm,),
        in_specs=[pl.BlockSpec((tm, k), lambda i: (i, 0)), pl.BlockSpec((k, n), lambda i: (0, 0))],
        out_specs=pl.BlockSpec((tm, n), lambda i: (i, 0)),
        compiler_params=_cparams(("parallel",)),
    )(a, b)


def _inproj_kernel(x_ref, sc_ref, sh_ref, w_ref, o_ref):
    h = x_ref[...] * (1.0 + sc_ref[...]) + sh_ref[...]
    o_ref[...] = jnp.dot(h.astype(BF16), w_ref[...], preferred_element_type=F32)


def in_projection(x, scale, shift, w, *, tm=512, tn=1024):
    g, l, d = x.shape
    n = w.shape[1]
    tm, tn = _pick(l, tm), _pick(n, tn)
    return pl.pallas_call(
        _inproj_kernel,
        out_shape=jax.ShapeDtypeStruct((g, l, n), F32),
        grid=(g, l // tm, n // tn),
        in_specs=[
            pl.BlockSpec((None, tm, d), lambda b, i, j: (b, i, 0)),
            pl.BlockSpec((None, 1, d), lambda b, i, j: (b, 0, 0)),
            pl.BlockSpec((None, 1, d), lambda b, i, j: (b, 0, 0)),
            pl.BlockSpec((d, tn), lambda b, i, j: (0, j)),
        ],
        out_specs=pl.BlockSpec((None, tm, tn), lambda b, i, j: (b, i, j)),
        compiler_params=_cparams(("parallel", "parallel", "arbitrary")),
    )(x, scale, shift, w)


def _attn_kernel(q_ref, k_ref, v_ref, o_ref):
    s = lax.dot_general(q_ref[...], k_ref[...], (((1,), (1,)), ((), ())),
                        preferred_element_type=F32) * SM_SCALE
    m = jnp.max(s, axis=-1, keepdims=True)
    p = jnp.exp(s - m)
    l = jnp.sum(p, axis=-1, keepdims=True)
    o = jnp.dot(p.astype(BF16), v_ref[...], preferred_element_type=F32)
    o_ref[...] = o / l


def attention(q, k, v, *, tq=256):
    b, h, l, dq = q.shape
    lk = k.shape[2]
    tq = _pick(l, tq)
    return pl.pallas_call(
        _attn_kernel,
        out_shape=jax.ShapeDtypeStruct((b, l, h * V_DIM), F32),
        grid=(b, h, l // tq),
        in_specs=[
            pl.BlockSpec((None, None, tq, dq), lambda bi, hi, i: (bi, hi, i, 0)),
            pl.BlockSpec((None, None, lk, dq), lambda bi, hi, i: (bi, hi, 0, 0)),
            pl.BlockSpec((None, None, lk, V_DIM), lambda bi, hi, i: (bi, hi, 0, 0)),
        ],
        out_specs=pl.BlockSpec((None, tq, V_DIM), lambda bi, hi, i: (bi, i, hi)),
        compiler_params=_cparams(("parallel", "parallel", "arbitrary")),
    )(q, k, v)


NEG_INF = float("-inf")


def _top_values(cur, count, store):
    for k in range(count):
        m = jnp.max(cur, axis=0, keepdims=True)
        store(k, m)
        cur = jnp.where(cur == m, NEG_INF, cur)


def _route_kernel(x_ref, sc_ref, sh_ref, wqt_ref, pk_ref, hb_ref, e1_ref, tau_ref, e2_ref,
                  qt_ref, s_ref, v_ref, fv_ref):
    tm = x_ref.shape[0]
    h = x_ref[...] * (1.0 + sc_ref[...]) + sh_ref[...]
    hb = h.astype(BF16)
    hb_ref[...] = hb
    qt = lax.dot_general(wqt_ref[...], hb, (((1,), (1,)), ((), ())), preferred_element_type=F32)
    for hp in range(2 * PEER_HEADS):
        qt_ref[hp] = qt[hp * PK_HALF:(hp + 1) * PK_HALF].astype(BF16)

    def head(hd, carry):
        for p in range(2):
            s = jnp.dot(pk_ref[2 * hd + p], qt_ref[2 * hd + p], preferred_element_type=F32)
            s_ref[p] = s

            def store(k, row, p=p):
                v_ref[p, k:k + 1, :] = row

            _top_values(s, PEER_TOPK, store)
        v2 = v_ref[1]
        comb = jnp.concatenate([v_ref[0, k:k + 1, :] + v2 for k in range(PEER_TOPK)], axis=0)

        def store_f(k, row):
            fv_ref[k:k + 1, :] = row

        _top_values(comb, PEER_TOPK + 1, store_f)
        top = fv_ref[0:1, :]
        z = jnp.sum(jnp.exp(fv_ref[0:PEER_TOPK, :] - top), axis=0, keepdims=True)
        thr = 0.5 * (fv_ref[PEER_TOPK - 1:PEER_TOPK, :] + fv_ref[PEER_TOPK:PEER_TOPK + 1, :])
        m1 = v_ref[0, 0:1, :]
        m2 = v_ref[1, 0:1, :]
        s1 = s_ref[0]
        e1_ref[hd] = jnp.exp(s1 - m1) / z
        e2_ref[hd] = jnp.exp(s_ref[1] - m2)
        tau_ref[hd] = jnp.exp((thr - m2) - s1)
        return carry

    lax.fori_loop(0, PEER_HEADS, head, 0)


def peer_route(x, scale, shift, wqt, pkeys, *, tm=256):
    g, l, d = x.shape
    t = g * l
    tm = _pick(l, tm)
    nl = l // tm
    fac = jax.ShapeDtypeStruct((PEER_HEADS, N_KEYS, t), F32)
    fspec = pl.BlockSpec((PEER_HEADS, N_KEYS, tm), lambda b, i: (0, 0, b * nl + i))
    return pl.pallas_call(
        _route_kernel,
        out_shape=(jax.ShapeDtypeStruct((t, d), BF16), fac, fac, fac),
        grid=(g, nl),
        in_specs=[
            pl.BlockSpec((None, tm, d), lambda b, i: (b, i, 0)),
            pl.BlockSpec((None, 1, d), lambda b, i: (b, 0, 0)),
            pl.BlockSpec((None, 1, d), lambda b, i: (b, 0, 0)),
            pl.BlockSpec(wqt.shape, lambda b, i: (0, 0)),
            pl.BlockSpec(pkeys.shape, lambda b, i: (0, 0, 0)),
        ],
        out_specs=(pl.BlockSpec((tm, d), lambda b, i: (b * nl + i, 0)), fspec, fspec, fspec),
        scratch_shapes=[
            pltpu.VMEM((2 * PEER_HEADS, PK_HALF, tm), BF16),
            pltpu.VMEM((2, N_KEYS, tm), F32),
            pltpu.VMEM((2, PEER_TOPK, tm), F32),
            pltpu.VMEM((24, tm), F32),
        ],
        compiler_params=_cparams(("parallel", "arbitrary")),
    )(x, scale, shift, wqt, pkeys)


GELU_C = math.sqrt(2.0 / math.pi)


def _gelu_tanh(a):
    return 0.5 * a * (1.0 + jnp.tanh(GELU_C * (a + 0.044715 * (a * a * a))))


def _experts_kernel(hb_ref, e1_ref, tau_ref, e2_ref, u_ref, vt_ref, o_ref, acc_ref, act_ref, p_ref,
                    *, rows, nc):
    c = pl.program_id(1)
    tm = hb_ref.shape[0]

    @pl.when(c == 0)
    def _():
        acc_ref[...] = jnp.zeros_like(acc_ref)

    act_ref[...] = lax.dot_general(u_ref[...], hb_ref[...], (((1,), (1,)), ((), ())),
                                   preferred_element_type=F32)

    def row_group(rg, carry):
        i0 = pl.multiple_of(c * rows + rg * SUBLANES, SUBLANES)
        for lt in range(tm // LANES):
            ls = slice(lt * LANES, (lt + 1) * LANES)
            taus = [tau_ref[hd, pl.ds(i0, SUBLANES), ls] for hd in range(PEER_HEADS)]
            e1s = [e1_ref[hd, pl.ds(i0, SUBLANES), ls] for hd in range(PEER_HEADS)]
            for j in range(SUBLANES):
                w = jnp.zeros((N_KEYS, LANES), F32)
                for hd in range(PEER_HEADS):
                    e2 = e2_ref[hd, :, ls]
                    w = w + jnp.where(e2 >= taus[hd][j:j + 1], e2, 0.0) * e1s[hd][j:j + 1]
                base = pl.multiple_of((rg * SUBLANES + j) * N_KEYS, N_KEYS)
                a = act_ref[pl.ds(base, N_KEYS), ls]
                p_ref[pl.ds(base, N_KEYS), ls] = (w * _gelu_tanh(a)).astype(BF16)
        return carry

    lax.fori_loop(0, rows // SUBLANES, row_group, 0)
    acc_ref[...] += jnp.dot(vt_ref[...], p_ref[...], preferred_element_type=F32)

    @pl.when(c == nc - 1)
    def _():
        o_ref[...] = acc_ref[...].T


def peer_experts(hb, e1, tau, e2, u, vt, *, tm=512, ec=1024):
    t, d = hb.shape
    tm = _pick(t, tm)
    nc = N_EXPERTS // ec
    rows = ec // N_KEYS
    fspec = pl.BlockSpec((PEER_HEADS, N_KEYS, tm), lambda i, c: (0, 0, i))
    return pl.pallas_call(
        functools.partial(_experts_kernel, rows=rows, nc=nc),
        out_shape=jax.ShapeDtypeStruct((t, d), F32),
        grid=(t // tm, nc),
        in_specs=[
            pl.BlockSpec((tm, d), lambda i, c: (i, 0)),
            fspec, fspec, fspec,
            pl.BlockSpec((ec, d), lambda i, c: (c, 0)),
            pl.BlockSpec((d, ec), lambda i, c: (0, c)),
        ],
        out_specs=pl.BlockSpec((tm, d), lambda i, c: (i, 0)),
        scratch_shapes=[
            pltpu.VMEM((d, tm), F32),
            pltpu.VMEM((ec, tm), F32),
            pltpu.VMEM((ec, tm), BF16),
        ],
        compiler_params=_cparams(("parallel", "arbitrary")),
    )(hb, e1, tau, e2, u, vt)


def _layer_norm(x, g, b):
    mu = jnp.mean(x, axis=-1, keepdims=True)
    var = jnp.mean(jnp.square(x - mu), axis=-1, keepdims=True)
    return (x - mu) * lax.rsqrt(var + LN_EPS) * g + b


def _rms_norm(x, g):
    return x * lax.rsqrt(jnp.mean(jnp.square(x), axis=-1, keepdims=True) + RMS_EPS) * g


def _rope_tables(l):
    rows = l // GRID_W
    row = jnp.repeat(jnp.arange(rows), GRID_W)
    col = jnp.tile(jnp.arange(GRID_W), rows)
    pos = jnp.stack([row, col], axis=-1).astype(F32)
    inv = ROPE_THETA ** (-jnp.arange(ROPE_PAIRS, dtype=F32) / ROPE_PAIRS)
    ang = pos[:, :, None] * inv
    return jnp.cos(ang), jnp.sin(ang)


def _apply_rope(x, cos, sin):
    b, l, h, _ = x.shape
    xr = x.reshape(b, l, h, 2, 2, ROPE_PAIRS)
    x1, x2 = xr[..., 0, :], xr[..., 1, :]
    c = cos[None, :, None]
    s = sin[None, :, None]
    out = jnp.stack([x1 * c - x2 * s, x2 * c + x1 * s], axis=-2)
    return out.reshape(b, l, h, QK_ROPE)


def _conv3(x, w):
    xp = jnp.pad(x, ((0, 0), (1, 1), (0, 0)))
    return xp[:, :-2] * w[0] + xp[:, 1:-1] * w[1] + xp[:, 2:] * w[2]


def _dft_matrices(l):
    n = 2 * l
    k = jnp.arange(l, dtype=jnp.int32)[:, None]
    t = jnp.arange(n, dtype=jnp.int32)[None, :]
    ang = ((k * t) % n).astype(F32) * (2.0 * math.pi / n)
    cos = jnp.cos(ang)
    msin = -jnp.sin(ang)
    nyq = jnp.where(t % 2 == 0, 1.0, -1.0).astype(F32)
    msin = jnp.where(k == 0, nyq, msin)
    fwd = jnp.concatenate([cos, msin], axis=0)
    wgt = jnp.where(k == 0, 1.0 / n, 2.0 / n)
    inv = jnp.concatenate([cos[:, :l] * wgt, msin[:, :l] * wgt], axis=0).T
    return fwd.astype(BF16), inv.astype(BF16)


def _hyena_filter_spectrum(l, p, fwd):
    t = jnp.linspace(0.0, 1.0, l, dtype=F32)[:, None]
    bands = (HY_EMB - 1) // 2
    w = 2.0 * math.pi * jnp.arange(l, dtype=F32) / l
    fr_b = jnp.linspace(1e-4, bands - 1, bands, dtype=F32)
    ang = w[:, None] * fr_b[None]
    z = jnp.concatenate([t, jnp.cos(ang), -jnp.sin(ang)], axis=-1)
    zp = jnp.pad(z, ((0, 0), (0, LANES - HY_EMB)))
    w1 = jnp.pad(p['hy_f_w1'], ((0, LANES - HY_EMB), (0, LANES - HY_HIDDEN)))
    w2 = jnp.pad(p['hy_f_w2'], ((0, LANES - HY_HIDDEN), (0, LANES - HY_HIDDEN)))
    w3 = jnp.pad(p['hy_f_w3'], ((0, LANES - HY_HIDDEN), (0, 0)))
    pad_h = (0, LANES - HY_HIDDEN)
    fr = jnp.pad(p['hy_f_freq'], pad_h)
    hid = jnp.sin(fr * (matmul_f32(zp, w1) + jnp.pad(p['hy_f_b1'], pad_h)))
    hid = jnp.sin(fr * (matmul_f32(hid, w2) + jnp.pad(p['hy_f_b2'], pad_h)))
    h = matmul_f32(hid, w3).reshape(l, 2, HY_ORDER, HY_W)
    h = h * jnp.exp(-t[:, :, None, None] * jnp.abs(p['hy_decay']))
    k = jnp.concatenate([h[:, 0], jnp.zeros((1, HY_ORDER, HY_W), F32), h[1:, 1][::-1]], axis=0)
    k = k / jnp.sum(jnp.abs(k), axis=0, keepdims=True)
    spec = matmul(fwd, k.reshape(2 * l, HY_ORDER * HY_W))
    kr, ki = spec[:l], spec[l:]
    first = (jnp.arange(l) == 0)[:, None]
    a = kr
    b = jnp.where(first, 0.0, ki)
    d = jnp.where(first, ki, kr)
    return a, b, d


def _long_conv(z, coef, fwd_half, inv, o):
    l = z.shape[1]
    a, b, d = (c[:, o * HY_W:(o + 1) * HY_W] for c in coef)
    spec = matmul(fwd_half, z.astype(BF16))
    xr, xi = spec[:, :l], spec[:, l:]
    y = jnp.concatenate([xr * a - xi * b, xr * b + xi * d], axis=1).astype(BF16)
    return matmul(inv, y)


def _hyena(u_in, p, dft, coef):
    fwd, inv = dft
    l = u_in.shape[1]
    u = _conv3(u_in, p['hy_sconv_w']) + p['hy_sconv_b']
    v, x1, x2 = jnp.split(u, 3, axis=-1)
    fwd_half = fwd[:, :l]
    z = v
    for o, gate in enumerate((x1, x2)):
        z = gate * (_long_conv(z, coef, fwd_half, inv, o) + p['hy_bias'][o] * z)
    return z


def _mixers(proj, p, dft, coef, ctx_ckv, ctx_krope, rope):
    b, l, _ = proj.shape
    q_lat = proj[..., :Q_RANK]
    kv_lat = proj[..., Q_RANK:KR_OFF]
    k_r = proj[..., KR_OFF:KR_OFF + QK_ROPE]
    sc_in = proj[..., SC_OFF:HY_OFF]
    hy_in = proj[..., HY_OFF:]
    qn = _rms_norm(q_lat, p['q_norm'])
    q = matmul(qn.reshape(b * l, Q_RANK), p['w_uq']).reshape(b, l, MLA_HEADS, QK_DIM)
    q_nope, q_rope = q[..., :QK_NOPE], q[..., QK_NOPE:]
    ckv = _rms_norm(kv_lat, p['kv_norm'])
    if rope is None:
        ckv_all, kr_all = ckv, k_r
    else:
        cos, sin = rope
        q_rope = _apply_rope(q_rope, cos, sin)
        kr_rot = _apply_rope(k_r[:, :, None], cos, sin)[:, :, 0]
        ckv_all = jnp.concatenate([ctx_ckv, ckv], axis=1)
        kr_all = jnp.concatenate([ctx_krope, kr_rot], axis=1)
    lk = ckv_all.shape[1]
    kv = matmul(ckv_all.reshape(b * lk, KV_RANK), p['w_ukv']).reshape(b, lk, MLA_HEADS, QK_NOPE + V_DIM)
    qf = jnp.concatenate([q_nope, q_rope], axis=-1).astype(BF16).transpose(0, 2, 1, 3)
    kf = jnp.concatenate(
        [kv[..., :QK_NOPE], jnp.broadcast_to(kr_all[:, :, None], (b, lk, MLA_HEADS, QK_ROPE))],
        axis=-1).astype(BF16).transpose(0, 2, 1, 3)
    vf = kv[..., QK_NOPE:].astype(BF16).transpose(0, 2, 1, 3)
    att = attention(qf, kf, vf)
    bg, cg, hs = jnp.split(sc_in, 3, axis=-1)
    sc = bg * _conv3(cg * hs, p['sc_conv'])
    hy = _hyena(hy_in, p, dft, coef)
    return jnp.concatenate([att, sc, hy], axis=-1), ckv, k_r


def _layer(x_all, p, cache_ckv_l, cache_krope_l, dfts, rope, n_ctx, ctx_len):
    g, lg, d = x_all.shape
    mod = p['mod']
    proj = in_projection(x_all, mod[:, 1:2], mod[:, 0:1], p['w_in'])
    coef_c = _hyena_filter_spectrum(ctx_len, p, dfts[0][0])
    coef_s = _hyena_filter_spectrum(lg, p, dfts[1][0])
    cat_c, ckv, k_r = _mixers(proj[0].reshape(n_ctx, ctx_len, -1), p, dfts[0], coef_c, None, None, None)
    cat_s, _, _ = _mixers(proj[1:], p, dfts[1], coef_s, cache_ckv_l, cache_krope_l, rope)
    cat = jnp.concatenate([cat_c.reshape(1, lg, d), cat_s], axis=0)
    mix = matmul(cat.reshape(g * lg, d), p['w_out'], tn=2048).reshape(g, lg, d)
    x1 = _layer_norm(ALPHA * x_all + mod[:, 2:3] * mix, p['ln1_g'], p['ln1_b'])
    hb, e1, tau, e2 = peer_route(x1, mod[:, 4:5], mod[:, 3:4], p['wqt'], p['pkeys'])
    ff = peer_experts(hb, e1, tau, e2, p['peer_u'], p['peer_vt']).reshape(g, lg, d)
    x2 = _layer_norm(ALPHA * x1 + mod[:, 5:6] * ff, p['ln2_g'], p['ln2_b'])
    return x2, ckv, k_r


def kernel(x_prompt, x_sample, cache_ckv, cache_krope, c, c_ctx, w_mod, b_mod, w_in, mla_q_norm, mla_kv_norm, w_uq, w_ukv, sc_conv, hy_sconv_w, hy_sconv_b, hy_f_w1, hy_f_b1, hy_f_w2, hy_f_b2, hy_f_w3, hy_f_freq, hy_decay, hy_bias, w_out, ln1_g, ln1_b, ln2_g, ln2_b, pk_query, pk_keys, peer_u, peer_v):
    n_ctx, ctx_len, d = x_prompt.shape
    n_lat, lat_len, _ = x_sample.shape
    depth = w_in.shape[0]
    assert n_ctx * ctx_len == lat_len
    g = n_lat + 1

    cond = jnp.concatenate([c_ctx[None], c], axis=0)
    cond = jnp.pad(jax.nn.silu(cond), ((0, 16 - g), (0, 0)))
    mod = matmul(cond, w_mod, tm=16, tn=1536)[:, :g] + b_mod[:, None]
    mod = mod.reshape(depth, g, N_MOD, d)

    w_in_p = jnp.concatenate(
        [w_in[..., :KR_OFF + QK_ROPE],
         jnp.zeros((depth, d, SC_OFF - KR_OFF - QK_ROPE), w_in.dtype),
         w_in[..., KR_OFF + QK_ROPE:]], axis=-1).astype(BF16)
    layers = {
        'mod': mod, 'w_in': w_in_p, 'q_norm': mla_q_norm, 'kv_norm': mla_kv_norm,
        'w_uq': w_uq.astype(BF16), 'w_ukv': w_ukv.astype(BF16), 'sc_conv': sc_conv,
        'hy_sconv_w': hy_sconv_w, 'hy_sconv_b': hy_sconv_b, 'hy_f_w1': hy_f_w1, 'hy_f_b1': hy_f_b1,
        'hy_f_w2': hy_f_w2, 'hy_f_b2': hy_f_b2, 'hy_f_w3': hy_f_w3, 'hy_f_freq': hy_f_freq,
        'hy_decay': hy_decay, 'hy_bias': hy_bias, 'w_out': w_out.astype(BF16),
        'ln1_g': ln1_g, 'ln1_b': ln1_b, 'ln2_g': ln2_g, 'ln2_b': ln2_b,
        'wqt': jnp.swapaxes(pk_query, 1, 2).astype(BF16),
        'pkeys': pk_keys.reshape(depth, 2 * PEER_HEADS, N_KEYS, PK_HALF).astype(BF16),
        'peer_u': peer_u.astype(BF16),
        'peer_vt': jnp.swapaxes(peer_v.astype(BF16), 1, 2),
        'cache_ckv': jnp.swapaxes(cache_ckv, 0, 1),
        'cache_krope': jnp.swapaxes(cache_krope, 0, 1),
    }
    dfts = (_dft_matrices(ctx_len), _dft_matrices(lat_len))
    rope = _rope_tables(lat_len)

    def step(x_all, p):
        x_new, ckv, k_r = _layer(x_all, p, p['cache_ckv'], p['cache_krope'], dfts, rope, n_ctx, ctx_len)
        return x_new, (ckv, k_r)

    x_all = jnp.concatenate([x_prompt.reshape(1, lat_len, d), x_sample], axis=0)
    x_all, (ckv, k_r) = lax.scan(step, x_all, layers)
    y_prompt = x_all[0].reshape(n_ctx, ctx_len, d)
    y_sample = x_all[1:]
    return y_prompt, y_sample, jnp.swapaxes(ckv, 0, 1), jnp.swapaxes(k_r, 0, 1)
```

```python
import functools
import math

import jax
import jax.numpy as jnp
from jax import lax
from jax.experimental import pallas as pl
from jax.experimental.pallas import tpu as pltpu

F32 = jnp.float32
BF16 = jnp.bfloat16

D_MODEL = 2048
DEPTH = 4
GRID_W = 64
MLA_HEADS = 8
QK_NOPE = 128
QK_ROPE = 64
QK_DIM = QK_NOPE + QK_ROPE
V_DIM = 128
Q_RANK = D_MODEL // 4
KV_RANK = D_MODEL // 8
ROPE_PAIRS = QK_ROPE // 4
ROPE_THETA = 10000.0
SC_W = D_MODEL // 4
HY_W = D_MODEL // 4
HY_ORDER = 2
HY_EMB = 33
HY_HIDDEN = 64
PEER_HEADS = 8
N_KEYS = 128
N_EXPERTS = N_KEYS * N_KEYS
PK_HALF = 128
PEER_TOPK = 16
N_MOD = 6
ALPHA = (2 * DEPTH) ** 0.25
SM_SCALE = QK_DIM ** -0.5
LN_EPS = 1e-5
RMS_EPS = 1e-6

KR_OFF = Q_RANK + KV_RANK
SC_OFF = 1024
HY_OFF = SC_OFF + 3 * SC_W
IN_COLS_PAD = HY_OFF + 3 * HY_W

VMEM_LIMIT = 52 * 1024 * 1024
LANES = 128
SUBLANES = 8


def _cparams(sem):
    return pltpu.CompilerParams(dimension_semantics=sem, vmem_limit_bytes=VMEM_LIMIT)


def _mm_kernel(a_ref, b_ref, o_ref, acc_ref, *, nk):
    part = jnp.dot(a_ref[...].astype(BF16), b_ref[...].astype(BF16), preferred_element_type=F32)
    if nk == 1:
        o_ref[...] = part.astype(o_ref.dtype)
        return
    k = pl.program_id(3)

    @pl.when(k == 0)
    def _():
        acc_ref[...] = part

    @pl.when(k > 0)
    def _():
        acc_ref[...] += part

    @pl.when(k == nk - 1)
    def _():
        o_ref[...] = acc_ref[...].astype(o_ref.dtype)


def _pick(n, pref):
    t = min(n, pref)
    while n % t:
        t //= 2
    return t


def matmul(a, b, *, tm=512, tn=512, tk=2048, out_dtype=F32):
    squeeze = a.ndim == 2 and b.ndim == 2
    if a.ndim == 2:
        a = a[None]
    if b.ndim == 2:
        b = b[None]
    ba, m, k = a.shape
    bb, k2, n = b.shape
    assert k == k2 and (ba == bb or 1 in (ba, bb))
    nb = max(ba, bb)
    tm, tn, tk = _pick(m, tm), _pick(n, tn), _pick(k, tk)
    nk = k // tk
    out = pl.pallas_call(
        functools.partial(_mm_kernel, nk=nk),
        out_shape=jax.ShapeDtypeStruct((nb, m, n), out_dtype),
        grid=(nb, m // tm, n // tn, nk),
        in_specs=[
            pl.BlockSpec((None, tm, tk), lambda g, i, j, kk: (g if ba > 1 else 0, i, kk)),
            pl.BlockSpec((None, tk, tn), lambda g, i, j, kk: (g if bb > 1 else 0, kk, j)),
        ],
        out_specs=pl.BlockSpec((None, tm, tn), lambda g, i, j, kk: (g, i, j)),
        scratch_shapes=[pltpu.VMEM((tm, tn), F32)],
        compiler_params=_cparams(("parallel", "parallel", "parallel", "arbitrary")),
    )(a, b)
    return out[0] if squeeze else out


def _mm_f32_kernel(a_ref, b_ref, o_ref):
    o_ref[...] = jnp.dot(a_ref[...], b_ref[...], preferred_element_type=F32,
                         precision=lax.Precision.HIGHEST)


def matmul_f32(a, b, *, tm=512):
    m, k = a.shape
    n = b.shape[1]
    tm = _pick(m, tm)
    return pl.pallas_call(
        _mm_f32_kernel,
        out_shape=jax.ShapeDtypeStruct((m, n), F32),
        grid=(m // tm,),
        in_specs=[pl.BlockSpec((tm, k), lambda i: (i, 0)), pl.BlockSpec((k, n), lambda i: (0, 0))],
        out_specs=pl.BlockSpec((tm, n), lambda i: (i, 0)),
        compiler_params=_cparams(("parallel",)),
    )(a, b)


def _inproj_kernel(x_ref, sc_ref, sh_ref, w_ref, o_ref):
    h = x_ref[...] * (1.0 + sc_ref[...]) + sh_ref[...]
    o_ref[...] = jnp.dot(h.astype(BF16), w_ref[...], preferred_element_type=F32)


def in_projection(x, scale, shift, w, *, tm=512, tn=2048):
    g, l, d = x.shape
    n = w.shape[1]
    tm, tn = _pick(l, tm), _pick(n, tn)
    return pl.pallas_call(
        _inproj_kernel,
        out_shape=jax.ShapeDtypeStruct((g, l, n), F32),
        grid=(g, l // tm, n // tn),
        in_specs=[
            pl.BlockSpec((None, tm, d), lambda b, i, j: (b, i, 0)),
            pl.BlockSpec((None, 1, d), lambda b, i, j: (b, 0, 0)),
            pl.BlockSpec((None, 1, d), lambda b, i, j: (b, 0, 0)),
            pl.BlockSpec((d, tn), lambda b, i, j: (0, j)),
        ],
        out_specs=pl.BlockSpec((None, tm, tn), lambda b, i, j: (b, i, j)),
        compiler_params=_cparams(("parallel", "parallel", "arbitrary")),
    )(x, scale, shift, w)


def _ln_rows(y, g, b):
    mu = jnp.mean(y, axis=-1, keepdims=True)
    yc = y - mu
    var = jnp.mean(yc * yc, axis=-1, keepdims=True)
    return yc * lax.rsqrt(var + LN_EPS) * g + b


def _outproj_kernel(cat_ref, x_ref, gate_ref, w_ref, g_ref, b_ref, o_ref):
    mix = jnp.dot(cat_ref[...].astype(BF16), w_ref[...], preferred_element_type=F32)
    o_ref[...] = _ln_rows(ALPHA * x_ref[...] + gate_ref[...] * mix, g_ref[...], b_ref[...])


def out_projection(cat, x, gate, w, ln_g, ln_b, *, tm=512):
    g, l, d = x.shape
    tm = _pick(l, tm)
    tok = pl.BlockSpec((None, tm, d), lambda b, i: (b, i, 0))
    vec = pl.BlockSpec((1, d), lambda b, i: (0, 0))
    return pl.pallas_call(
        _outproj_kernel,
        out_shape=jax.ShapeDtypeStruct((g, l, d), F32),
        grid=(g, l // tm),
        in_specs=[tok, tok, pl.BlockSpec((None, 1, d), lambda b, i: (b, 0, 0)),
                  pl.BlockSpec((d, d), lambda b, i: (0, 0)), vec, vec],
        out_specs=tok,
        compiler_params=_cparams(("parallel", "parallel")),
    )(cat, x, gate, w, ln_g.reshape(1, d), ln_b.reshape(1, d))


def _attn_kernel(q_ref, k_ref, v_ref, o_ref):
    s = lax.dot_general(q_ref[...], k_ref[...], (((1,), (1,)), ((), ())),
                        preferred_element_type=F32)
    m = jnp.max(s, axis=-1, keepdims=True)
    p = jnp.exp(s - m)
    l = jnp.sum(p, axis=-1, keepdims=True)
    o = jnp.dot(p.astype(BF16), v_ref[...], preferred_element_type=F32)
    o_ref[...] = o / l


def attention(q, k, v, *, tq=256):
    b, h, l, dq = q.shape
    lk = k.shape[2]
    tq = _pick(l, tq)
    return pl.pallas_call(
        _attn_kernel,
        out_shape=jax.ShapeDtypeStruct((b, l, h * V_DIM), F32),
        grid=(b, h, l // tq),
        in_specs=[
            pl.BlockSpec((None, None, tq, dq), lambda bi, hi, i: (bi, hi, i, 0)),
            pl.BlockSpec((None, None, lk, dq), lambda bi, hi, i: (bi, hi, 0, 0)),
            pl.BlockSpec((None, None, lk, V_DIM), lambda bi, hi, i: (bi, hi, 0, 0)),
        ],
        out_specs=pl.BlockSpec((None, tq, V_DIM), lambda bi, hi, i: (bi, i, hi)),
        compiler_params=_cparams(("parallel", "parallel", "arbitrary")),
    )(q, k, v)


NEG_INF = float("-inf")


def _exchange(xs, i, j):
    xs[i], xs[j] = jnp.maximum(xs[i], xs[j]), jnp.minimum(xs[i], xs[j])


def _bitonic_merge_desc(xs):
    n = len(xs)
    j = n // 2
    while j >= 1:
        for i in range(n):
            if i & j == 0:
                _exchange(xs, i, i | j)
        j //= 2


def _sort_desc(xs):
    n = len(xs)
    k = 2
    while k <= n:
        j = k // 2
        while j >= 1:
            for i in range(n):
                l = i ^ j
                if l > i:
                    if i & k == 0:
                        _exchange(xs, i, l)
                    else:
                        _exchange(xs, l, i)
            j //= 2
        k *= 2


def _all_sublanes_max(m):
    for shift in (4, 2, 1):
        m = jnp.maximum(m, pltpu.roll(m, shift, 0))
    return m


def _top16_sorted(s):
    xs = [s[v * SUBLANES:(v + 1) * SUBLANES] for v in range(N_KEYS // SUBLANES)]
    _sort_desc(xs)
    for shift in (4, 2, 1):
        xs = [jnp.maximum(xs[r], pltpu.roll(xs[PEER_TOPK - 1 - r], shift, 0)) for r in range(PEER_TOPK)]
        _bitonic_merge_desc(xs)
    return xs


def _spread(rows, sub):
    out = rows[0]
    for b in range(1, SUBLANES):
        out = jnp.where(sub == b, rows[b], out)
    return out


def _all_sublanes_min(m):
    for shift in (4, 2, 1):
        m = jnp.minimum(m, pltpu.roll(m, shift, 0))
    return m


def _best_sums(cands, sub, count):
    ids = [sub + g * SUBLANES for g in range(len(cands))]
    far = len(cands) * SUBLANES
    best = []
    for _ in range(count):
        m = cands[0]
        for cnd in cands[1:]:
            m = jnp.maximum(m, cnd)
        m = _all_sublanes_max(m)
        best.append(m)
        first = jnp.where(cands[0] == m, ids[0], far)
        for cnd, idg in zip(cands[1:], ids[1:]):
            first = jnp.minimum(first, jnp.where(cnd == m, idg, far))
        first = _all_sublanes_min(first)
        cands = [jnp.where(idg == first, NEG_INF, cnd) for cnd, idg in zip(cands, ids)]
    return best


def _route_kernel(x_ref, sc_ref, sh_ref, wqt_ref, pk_ref, hb_ref, e1_ref, tau_ref, e2_ref, qt_ref, s_ref):
    tm = x_ref.shape[0]
    h = x_ref[...] * (1.0 + sc_ref[...]) + sh_ref[...]
    hb = h.astype(BF16)
    hb_ref[...] = hb
    qt = lax.dot_general(wqt_ref[...], hb, (((1,), (1,)), ((), ())), preferred_element_type=F32)
    for hp in range(2 * PEER_HEADS):
        qt_ref[hp] = qt[hp * PK_HALF:(hp + 1) * PK_HALF].astype(BF16)

    def head(hd, carry):
        for p in range(2):
            s_ref[p] = jnp.dot(pk_ref[2 * hd + p], qt_ref[2 * hd + p], preferred_element_type=F32)
        for lt in range(tm // LANES):
            ls = slice(lt * LANES, (lt + 1) * LANES)
            s1 = s_ref[0, :, ls]
            s2 = s_ref[1, :, ls]
            v1 = _top16_sorted(s1)
            v2 = _top16_sorted(s2)
            sub = lax.broadcasted_iota(jnp.int32, (SUBLANES, LANES), 0)
            v2_lo, v2_hi = _spread(v2[:SUBLANES], sub), _spread(v2[SUBLANES:], sub)
            v1_hi = _spread(v1[SUBLANES:], sub)
            cands = [v1[0] + v2_lo, v1[0] + v2_hi, v1_hi + v2[0]]
            cands += [v1[a] + v2_lo for a in range(1, SUBLANES)]
            best = _best_sums(cands, sub, PEER_TOPK + 1)
            z = jnp.ones_like(best[0])
            for k in range(1, PEER_TOPK):
                z = z + jnp.exp(best[k] - best[0])
            thr = 0.5 * (best[PEER_TOPK - 1] + best[PEER_TOPK])
            m1 = v1[0][0:1]
            m2 = v2[0][0:1]
            in1 = s1 >= v1[PEER_TOPK - 1][0:1]
            in2 = s2 >= v2[PEER_TOPK - 1][0:1]
            e1_ref[hd, :, ls] = jnp.exp(s1 - m1) / z[0:1]
            e2_ref[hd, :, ls] = jnp.where(in2, jnp.exp(s2 - m2), 0.0)
            tau_ref[hd, :, ls] = jnp.where(in1, jnp.exp((thr[0:1] - m2) - s1), jnp.inf)
        return carry

    lax.fori_loop(0, PEER_HEADS, head, 0)


def peer_route(x, scale, shift, wqt, pkeys, *, tm=256):
    g, l, d = x.shape
    t = g * l
    tm = _pick(l, tm)
    nl = l // tm
    fac = jax.ShapeDtypeStruct((PEER_HEADS, N_KEYS, t), F32)
    fspec = pl.BlockSpec((PEER_HEADS, N_KEYS, tm), lambda b, i: (0, 0, b * nl + i))
    return pl.pallas_call(
        _route_kernel,
        out_shape=(jax.ShapeDtypeStruct((g, l, d), BF16), fac, fac, fac),
        grid=(g, nl),
        in_specs=[
            pl.BlockSpec((None, tm, d), lambda b, i: (b, i, 0)),
            pl.BlockSpec((None, 1, d), lambda b, i: (b, 0, 0)),
            pl.BlockSpec((None, 1, d), lambda b, i: (b, 0, 0)),
            pl.BlockSpec(wqt.shape, lambda b, i: (0, 0)),
            pl.BlockSpec(pkeys.shape, lambda b, i: (0, 0, 0)),
        ],
        out_specs=(pl.BlockSpec((None, tm, d), lambda b, i: (b, i, 0)), fspec, fspec, fspec),
        scratch_shapes=[
            pltpu.VMEM((2 * PEER_HEADS, PK_HALF, tm), BF16),
            pltpu.VMEM((2, N_KEYS, tm), F32),
        ],
        compiler_params=_cparams(("parallel", "arbitrary")),
    )(x, scale, shift, wqt, pkeys)


GELU_C = math.sqrt(2.0 / math.pi)


def _gelu_tanh(a):
    half = 0.5 * a
    return half + half * jnp.tanh(a * (GELU_C + (GELU_C * 0.044715) * (a * a)))


EXPERT_CHUNK = SUBLANES * N_KEYS


def _experts_kernel(hb_ref, e1_ref, tau_ref, e2_ref, u_ref, vt_ref, x_ref, gate_ref, g_ref, b_ref,
                    o_ref, acc_ref, act_ref, p_ref, *, nc):
    c = pl.program_id(2)
    tm = hb_ref.shape[0]

    @pl.when(c == 0)
    def _():
        acc_ref[...] = jnp.zeros_like(acc_ref)

    act_ref[...] = lax.dot_general(u_ref[...], hb_ref[...], (((1,), (1,)), ((), ())),
                                   preferred_element_type=F32)
    for lt in range(tm // LANES):
        ls = slice(lt * LANES, (lt + 1) * LANES)
        for j in range(SUBLANES):
            w = jnp.zeros((N_KEYS, LANES), F32)
            for hd in range(PEER_HEADS):
                e2 = e2_ref[hd, :, ls]
                w = w + jnp.where(e2 >= tau_ref[hd, j:j + 1, ls], e2, 0.0) * e1_ref[hd, j:j + 1, ls]
            rs = slice(j * N_KEYS, (j + 1) * N_KEYS)
            p_ref[rs, ls] = (w * _gelu_tanh(act_ref[rs, ls])).astype(BF16)
    acc_ref[...] += jnp.dot(vt_ref[...], p_ref[...], preferred_element_type=F32)

    @pl.when(c == nc - 1)
    def _():
        y = ALPHA * x_ref[...] + gate_ref[...] * acc_ref[...].T
        o_ref[...] = _ln_rows(y, g_ref[...], b_ref[...])


def peer_experts(hb, e1, tau, e2, u, vt, x, gate, ln_g, ln_b, *, tm=512):
    g, l, d = x.shape
    tm = _pick(l, tm)
    nl = l // tm
    ec = EXPERT_CHUNK
    nc = N_EXPERTS // ec
    rspec = pl.BlockSpec((PEER_HEADS, SUBLANES, tm), lambda b, i, c: (0, c, b * nl + i))
    once = pl.Buffered(1)
    fspec = pl.BlockSpec((PEER_HEADS, N_KEYS, tm), lambda b, i, c: (0, 0, b * nl + i), pipeline_mode=once)
    tok_in = pl.BlockSpec((None, tm, d), lambda b, i, c: (b, i, 0), pipeline_mode=once)
    tok = pl.BlockSpec((None, tm, d), lambda b, i, c: (b, i, 0))
    vec = pl.BlockSpec((1, d), lambda b, i, c: (0, 0))
    return pl.pallas_call(
        functools.partial(_experts_kernel, nc=nc),
        out_shape=jax.ShapeDtypeStruct((g, l, d), F32),
        grid=(g, nl, nc),
        in_specs=[
            tok_in, rspec, rspec, fspec,
            pl.BlockSpec((ec, d), lambda b, i, c: (c, 0)),
            pl.BlockSpec((d, ec), lambda b, i, c: (0, c)),
            tok_in, pl.BlockSpec((None, 1, d), lambda b, i, c: (b, 0, 0)), vec, vec,
        ],
        out_specs=tok,
        scratch_shapes=[
            pltpu.VMEM((d, tm), F32),
            pltpu.VMEM((ec, tm), F32),
            pltpu.VMEM((ec, tm), BF16),
        ],
        compiler_params=_cparams(("parallel", "parallel", "arbitrary")),
    )(hb, e1, tau, e2, u, vt, x, gate, ln_g.reshape(1, d), ln_b.reshape(1, d))


def _layer_norm(x, g, b):
    mu = jnp.mean(x, axis=-1, keepdims=True)
    var = jnp.mean(jnp.square(x - mu), axis=-1, keepdims=True)
    return (x - mu) * lax.rsqrt(var + LN_EPS) * g + b


def _rms_norm(x, g):
    return x * lax.rsqrt(jnp.mean(jnp.square(x), axis=-1, keepdims=True) + RMS_EPS) * g


def _rope_tables(l):
    rows = l // GRID_W
    row = jnp.repeat(jnp.arange(rows), GRID_W)
    col = jnp.tile(jnp.arange(GRID_W), rows)
    pos = jnp.stack([row, col], axis=-1).astype(F32)
    inv = ROPE_THETA ** (-jnp.arange(ROPE_PAIRS, dtype=F32) / ROPE_PAIRS)
    ang = pos[:, :, None] * inv
    return jnp.cos(ang), jnp.sin(ang)


def _apply_rope(x, cos, sin):
    b, l, h, _ = x.shape
    xr = x.reshape(b, l, h, 2, 2, ROPE_PAIRS)
    x1, x2 = xr[..., 0, :], xr[..., 1, :]
    c = cos[None, :, None]
    s = sin[None, :, None]
    out = jnp.stack([x1 * c - x2 * s, x2 * c + x1 * s], axis=-2)
    return out.reshape(b, l, h, QK_ROPE)


def _conv3(x, w):
    xp = jnp.pad(x, ((0, 0), (1, 1), (0, 0)))
    return xp[:, :-2] * w[0] + xp[:, 1:-1] * w[1] + xp[:, 2:] * w[2]


def _dft_matrices(l):
    n = 2 * l
    k = jnp.arange(l, dtype=jnp.int32)[:, None]
    t = jnp.arange(n, dtype=jnp.int32)[None, :]
    ang = ((k * t) % n).astype(F32) * (2.0 * math.pi / n)
    cos = jnp.cos(ang)
    msin = -jnp.sin(ang)
    nyq = jnp.where(t % 2 == 0, 1.0, -1.0).astype(F32)
    msin = jnp.where(k == 0, nyq, msin)
    fwd = jnp.concatenate([cos, msin], axis=0)
    wgt = jnp.where(k == 0, 1.0 / n, 2.0 / n)
    inv = jnp.concatenate([cos[:, :l] * wgt, msin[:, :l] * wgt], axis=0).T
    fwd = fwd.astype(BF16)
    return fwd, fwd[:, :l], inv.astype(BF16)


def _hyena_filter_spectrum(l, p, fwd):
    t = jnp.linspace(0.0, 1.0, l, dtype=F32)[:, None]
    bands = (HY_EMB - 1) // 2
    w = 2.0 * math.pi * jnp.arange(l, dtype=F32) / l
    fr_b = jnp.linspace(1e-4, bands - 1, bands, dtype=F32)
    ang = w[:, None] * fr_b[None]
    z = jnp.concatenate([t, jnp.cos(ang), -jnp.sin(ang)], axis=-1)
    zp = jnp.pad(z, ((0, 0), (0, LANES - HY_EMB)))
    w1 = jnp.pad(p['hy_f_w1'], ((0, LANES - HY_EMB), (0, LANES - HY_HIDDEN)))
    w2 = jnp.pad(p['hy_f_w2'], ((0, LANES - HY_HIDDEN), (0, LANES - HY_HIDDEN)))
    w3 = jnp.pad(p['hy_f_w3'], ((0, LANES - HY_HIDDEN), (0, 0)))
    pad_h = (0, LANES - HY_HIDDEN)
    fr = jnp.pad(p['hy_f_freq'], pad_h)
    hid = jnp.sin(fr * (matmul_f32(zp, w1) + jnp.pad(p['hy_f_b1'], pad_h)))
    hid = jnp.sin(fr * (matmul_f32(hid, w2) + jnp.pad(p['hy_f_b2'], pad_h)))
    h = matmul_f32(hid, w3).reshape(l, 2, HY_ORDER, HY_W)
    h = h * jnp.exp(-t[:, :, None, None] * jnp.abs(p['hy_decay']))
    k = jnp.concatenate([h[:, 0], jnp.zeros((1, HY_ORDER, HY_W), F32), h[1:, 1][::-1]], axis=0)
    k = k / jnp.sum(jnp.abs(k), axis=0, keepdims=True)
    spec = matmul(fwd, k.reshape(2 * l, HY_ORDER * HY_W))
    kr, ki = spec[:l], spec[l:]
    first = (jnp.arange(l) == 0)[:, None]
    a = kr
    b = jnp.where(first, 0.0, ki)
    d = jnp.where(first, ki, kr)
    return a, b, d


def _long_conv(z, coef, fwd_half, inv, o):
    l = z.shape[1]
    a, b, d = (c[:, o * HY_W:(o + 1) * HY_W] for c in coef)
    spec = matmul(fwd_half, z.astype(BF16), tm=1024)
    xr, xi = spec[:, :l], spec[:, l:]
    y = jnp.concatenate([xr * a - xi * b, xr * b + xi * d], axis=1).astype(BF16)
    return matmul(inv, y, tm=1024)


def _hyena(u_in, p, dft, coef):
    _, fwd_half, inv = dft
    u = _conv3(u_in, p['hy_sconv_w']) + p['hy_sconv_b']
    v, x1, x2 = jnp.split(u, 3, axis=-1)
    z = v
    for o, gate in enumerate((x1, x2)):
        z = gate * (_long_conv(z, coef, fwd_half, inv, o) + p['hy_bias'][o] * z)
    return z


def _mixers(proj, p, dft, coef, ctx_ckv, ctx_krope, rope):
    b, l, _ = proj.shape
    q_lat = proj[..., :Q_RANK]
    kv_lat = proj[..., Q_RANK:KR_OFF]
    k_r = proj[..., KR_OFF:KR_OFF + QK_ROPE]
    sc_in = proj[..., SC_OFF:HY_OFF]
    hy_in = proj[..., HY_OFF:]
    qn = _rms_norm(q_lat, p['q_norm'])
    q = matmul(qn.reshape(b * l, Q_RANK), p['w_uq']).reshape(b, l, MLA_HEADS, QK_DIM)
    q_nope, q_rope = q[..., :QK_NOPE], q[..., QK_NOPE:]
    ckv = _rms_norm(kv_lat, p['kv_norm'])
    if rope is None:
        ckv_all, kr_all = ckv, k_r
    else:
        cos, sin = rope
        q_rope = _apply_rope(q_rope, cos, sin)
        kr_rot = _apply_rope(k_r[:, :, None], cos, sin)[:, :, 0]
        ckv_all = jnp.concatenate([ctx_ckv, ckv], axis=1)
        kr_all = jnp.concatenate([ctx_krope, kr_rot], axis=1)
    lk = ckv_all.shape[1]
    kv = matmul(ckv_all.reshape(b * lk, KV_RANK), p['w_ukv']).reshape(b, lk, MLA_HEADS, QK_NOPE + V_DIM)
    qf = (jnp.concatenate([q_nope, q_rope], axis=-1) * SM_SCALE).astype(BF16).transpose(0, 2, 1, 3)
    kf = jnp.concatenate(
        [kv[..., :QK_NOPE], jnp.broadcast_to(kr_all[:, :, None], (b, lk, MLA_HEADS, QK_ROPE))],
        axis=-1).astype(BF16).transpose(0, 2, 1, 3)
    vf = kv[..., QK_NOPE:].astype(BF16).transpose(0, 2, 1, 3)
    att = attention(qf, kf, vf)
    bg, cg, hs = jnp.split(sc_in, 3, axis=-1)
    sc = bg * _conv3(cg * hs, p['sc_conv'])
    hy = _hyena(hy_in, p, dft, coef)
    return jnp.concatenate([att, sc, hy], axis=-1), ckv, k_r


def _layer(x_all, p, cache_ckv_l, cache_krope_l, dfts, rope, n_ctx, ctx_len):
    g, lg, d = x_all.shape
    mod = p['mod']
    proj = in_projection(x_all, mod[:, 1:2], mod[:, 0:1], p['w_in'])
    coef_c = _hyena_filter_spectrum(ctx_len, p, dfts[0][0])
    coef_s = _hyena_filter_spectrum(lg, p, dfts[1][0])
    cat_c, ckv, k_r = _mixers(proj[0].reshape(n_ctx, ctx_len, -1), p, dfts[0], coef_c, None, None, None)
    cat_s, _, _ = _mixers(proj[1:], p, dfts[1], coef_s, cache_ckv_l, cache_krope_l, rope)
    cat = jnp.concatenate([cat_c.reshape(1, lg, d), cat_s], axis=0)
    x1 = out_projection(cat, x_all, mod[:, 2:3], p['w_out'], p['ln1_g'], p['ln1_b'])
    hb, e1, tau, e2 = peer_route(x1, mod[:, 4:5], mod[:, 3:4], p['wqt'], p['pkeys'])
    x2 = peer_experts(hb, e1, tau, e2, p['peer_u'], p['peer_vt'], x1, mod[:, 5:6], p['ln2_g'], p['ln2_b'])
    return x2, ckv, k_r


def kernel(x_prompt, x_sample, cache_ckv, cache_krope, c, c_ctx, w_mod, b_mod, w_in, mla_q_norm, mla_kv_norm, w_uq, w_ukv, sc_conv, hy_sconv_w, hy_sconv_b, hy_f_w1, hy_f_b1, hy_f_w2, hy_f_b2, hy_f_w3, hy_f_freq, hy_decay, hy_bias, w_out, ln1_g, ln1_b, ln2_g, ln2_b, pk_query, pk_keys, peer_u, peer_v):
    n_ctx, ctx_len, d = x_prompt.shape
    n_lat, lat_len, _ = x_sample.shape
    depth = w_in.shape[0]
    assert n_ctx * ctx_len == lat_len
    g = n_lat + 1

    cond = jnp.concatenate([c_ctx[None], c], axis=0)
    cond = jnp.pad(jax.nn.silu(cond), ((0, 16 - g), (0, 0)))
    mod = matmul(cond, w_mod, tm=16, tn=1536)[:, :g] + b_mod[:, None]
    mod = mod.reshape(depth, g, N_MOD, d)

    w_in_p = jnp.concatenate(
        [w_in[..., :KR_OFF + QK_ROPE],
         jnp.zeros((depth, d, SC_OFF - KR_OFF - QK_ROPE), w_in.dtype),
         w_in[..., KR_OFF + QK_ROPE:]], axis=-1).astype(BF16)
    layers = {
        'mod': mod, 'w_in': w_in_p, 'q_norm': mla_q_norm, 'kv_norm': mla_kv_norm,
        'w_uq': w_uq.astype(BF16), 'w_ukv': w_ukv.astype(BF16), 'sc_conv': sc_conv,
        'hy_sconv_w': hy_sconv_w, 'hy_sconv_b': hy_sconv_b, 'hy_f_w1': hy_f_w1, 'hy_f_b1': hy_f_b1,
        'hy_f_w2': hy_f_w2, 'hy_f_b2': hy_f_b2, 'hy_f_w3': hy_f_w3, 'hy_f_freq': hy_f_freq,
        'hy_decay': hy_decay, 'hy_bias': hy_bias, 'w_out': w_out.astype(BF16),
        'ln1_g': ln1_g, 'ln1_b': ln1_b, 'ln2_g': ln2_g, 'ln2_b': ln2_b,
        'wqt': jnp.swapaxes(pk_query, 1, 2).astype(BF16),
        'pkeys': pk_keys.reshape(depth, 2 * PEER_HEADS, N_KEYS, PK_HALF).astype(BF16),
        'peer_u': peer_u.astype(BF16),
        'peer_vt': jnp.swapaxes(peer_v.astype(BF16), 1, 2),
        'cache_ckv': jnp.swapaxes(cache_ckv, 0, 1),
        'cache_krope': jnp.swapaxes(cache_krope, 0, 1),
    }
    dfts = (_dft_matrices(ctx_len), _dft_matrices(lat_len))
    rope = _rope_tables(lat_len)

    def step(x_all, p):
        x_new, ckv, k_r = _layer(x_all, p, p['cache_ckv'], p['cache_krope'], dfts, rope, n_ctx, ctx_len)
        return x_new, (ckv, k_r)

    x_all = jnp.concatenate([x_prompt.reshape(1, lat_len, d), x_sample], axis=0)
    x_all, (ckv, k_r) = lax.scan(step, x_all, layers)
    y_prompt = x_all[0].reshape(n_ctx, ctx_len, d)
    y_sample = x_all[1:]
    return y_prompt, y_sample, jnp.swapaxes(ckv, 0, 1), jnp.swapaxes(k_r, 0, 1)
```

```python
import functools
import math

import jax
import jax.numpy as jnp
from jax import lax
from jax.experimental import pallas as pl
from jax.experimental.pallas import tpu as pltpu

F32 = jnp.float32
BF16 = jnp.bfloat16

D_MODEL = 2048
DEPTH = 4
GRID_W = 64
MLA_HEADS = 8
QK_NOPE = 128
QK_ROPE = 64
QK_DIM = QK_NOPE + QK_ROPE
V_DIM = 128
Q_RANK = D_MODEL // 4
KV_RANK = D_MODEL // 8
ROPE_PAIRS = QK_ROPE // 4
ROPE_THETA = 10000.0
SC_W = D_MODEL // 4
HY_W = D_MODEL // 4
HY_ORDER = 2
HY_EMB = 33
HY_HIDDEN = 64
PEER_HEADS = 8
N_KEYS = 128
N_EXPERTS = N_KEYS * N_KEYS
PK_HALF = 128
PEER_TOPK = 16
N_MOD = 6
ALPHA = (2 * DEPTH) ** 0.25
SM_SCALE = QK_DIM ** -0.5
LN_EPS = 1e-5
RMS_EPS = 1e-6

KR_OFF = Q_RANK + KV_RANK
SC_OFF = 1024
HY_OFF = SC_OFF + 3 * SC_W
IN_COLS_PAD = HY_OFF + 3 * HY_W

VMEM_LIMIT = 52 * 1024 * 1024
LANES = 128
SUBLANES = 8


def _cparams(sem):
    return pltpu.CompilerParams(dimension_semantics=sem, vmem_limit_bytes=VMEM_LIMIT)


def _mm_kernel(a_ref, b_ref, o_ref, acc_ref, *, nk):
    part = jnp.dot(a_ref[...].astype(BF16), b_ref[...].astype(BF16), preferred_element_type=F32)
    if nk == 1:
        o_ref[...] = part.astype(o_ref.dtype)
        return
    k = pl.program_id(3)

    @pl.when(k == 0)
    def _():
        acc_ref[...] = part

    @pl.when(k > 0)
    def _():
        acc_ref[...] += part

    @pl.when(k == nk - 1)
    def _():
        o_ref[...] = acc_ref[...].astype(o_ref.dtype)


def _pick(n, pref):
    t = min(n, pref)
    while n % t:
        t //= 2
    return t


def matmul(a, b, *, tm=512, tn=512, tk=2048, out_dtype=F32):
    squeeze = a.ndim == 2 and b.ndim == 2
    if a.ndim == 2:
        a = a[None]
    if b.ndim == 2:
        b = b[None]
    ba, m, k = a.shape
    bb, k2, n = b.shape
    assert k == k2 and (ba == bb or 1 in (ba, bb))
    nb = max(ba, bb)
    tm, tn, tk = _pick(m, tm), _pick(n, tn), _pick(k, tk)
    nk = k // tk
    out = pl.pallas_call(
        functools.partial(_mm_kernel, nk=nk),
        out_shape=jax.ShapeDtypeStruct((nb, m, n), out_dtype),
        grid=(nb, m // tm, n // tn, nk),
        in_specs=[
            pl.BlockSpec((None, tm, tk), lambda g, i, j, kk: (g if ba > 1 else 0, i, kk)),
            pl.BlockSpec((None, tk, tn), lambda g, i, j, kk: (g if bb > 1 else 0, kk, j)),
        ],
        out_specs=pl.BlockSpec((None, tm, tn), lambda g, i, j, kk: (g, i, j)),
        scratch_shapes=[pltpu.VMEM((tm, tn), F32)],
        compiler_params=_cparams(("parallel", "parallel", "parallel", "arbitrary")),
    )(a, b)
    return out[0] if squeeze else out


def _mm_f32_kernel(a_ref, b_ref, o_ref):
    o_ref[...] = jnp.dot(a_ref[...], b_ref[...], preferred_element_type=F32,
                         precision=lax.Precision.HIGHEST)


def matmul_f32(a, b, *, tm=512):
    m, k = a.shape
    n = b.shape[1]
    tm = _pick(m, tm)
    return pl.pallas_call(
        _mm_f32_kernel,
        out_shape=jax.ShapeDtypeStruct((m, n), F32),
        grid=(m // tm,),
        in_specs=[pl.BlockSpec((tm, k), lambda i: (i, 0)), pl.BlockSpec((k, n), lambda i: (0, 0))],
        out_specs=pl.BlockSpec((tm, n), lambda i: (i, 0)),
        compiler_params=_cparams(("parallel",)),
    )(a, b)


def _inproj_kernel(x_ref, sc_ref, sh_ref, w_ref, o_ref):
    h = x_ref[...] * (1.0 + sc_ref[...]) + sh_ref[...]
    o_ref[...] = jnp.dot(h.astype(BF16), w_ref[...], preferred_element_type=F32)


def in_projection(x, scale, shift, w, *, tm=512, tn=2048):
    g, l, d = x.shape
    n = w.shape[1]
    tm, tn = _pick(l, tm), _pick(n, tn)
    return pl.pallas_call(
        _inproj_kernel,
        out_shape=jax.ShapeDtypeStruct((g, l, n), F32),
        grid=(g, l // tm, n // tn),
        in_specs=[
            pl.BlockSpec((None, tm, d), lambda b, i, j: (b, i, 0)),
            pl.BlockSpec((None, 1, d), lambda b, i, j: (b, 0, 0)),
            pl.BlockSpec((None, 1, d), lambda b, i, j: (b, 0, 0)),
            pl.BlockSpec((d, tn), lambda b, i, j: (0, j)),
        ],
        out_specs=pl.BlockSpec((None, tm, tn), lambda b, i, j: (b, i, j)),
        compiler_params=_cparams(("parallel", "parallel", "arbitrary")),
    )(x, scale, shift, w)


def _ln_rows(y, g, b):
    mu = jnp.mean(y, axis=-1, keepdims=True)
    yc = y - mu
    var = jnp.mean(yc * yc, axis=-1, keepdims=True)
    return yc * lax.rsqrt(var + LN_EPS) * g + b


def _outproj_kernel(cat_ref, x_ref, gate_ref, w_ref, g_ref, b_ref, o_ref):
    mix = jnp.dot(cat_ref[...].astype(BF16), w_ref[...], preferred_element_type=F32)
    o_ref[...] = _ln_rows(ALPHA * x_ref[...] + gate_ref[...] * mix, g_ref[...], b_ref[...])


def out_projection(cat, x, gate, w, ln_g, ln_b, *, tm=512):
    g, l, d = x.shape
    tm = _pick(l, tm)
    tok = pl.BlockSpec((None, tm, d), lambda b, i: (b, i, 0))
    vec = pl.BlockSpec((1, d), lambda b, i: (0, 0))
    return pl.pallas_call(
        _outproj_kernel,
        out_shape=jax.ShapeDtypeStruct((g, l, d), F32),
        grid=(g, l // tm),
        in_specs=[tok, tok, pl.BlockSpec((None, 1, d), lambda b, i: (b, 0, 0)),
                  pl.BlockSpec((d, d), lambda b, i: (0, 0)), vec, vec],
        out_specs=tok,
        compiler_params=_cparams(("parallel", "parallel")),
    )(cat, x, gate, w, ln_g.reshape(1, d), ln_b.reshape(1, d))


NOPE_W = MLA_HEADS * QK_NOPE
ROPE_W = MLA_HEADS * QK_ROPE


def _qproj_kernel(p_ref, g_ref, w_ref, c_ref, s_ref, o_ref):
    x = p_ref[...]
    xn = x * lax.rsqrt(jnp.mean(x * x, axis=-1, keepdims=True) + RMS_EPS) * g_ref[...]
    acc = jnp.dot(xn.astype(BF16), w_ref[...], preferred_element_type=F32)
    rope = acc[:, NOPE_W:NOPE_W + ROPE_W] * c_ref[...] + acc[:, NOPE_W + ROPE_W:] * s_ref[...]
    for h in range(MLA_HEADS):
        o_ref[h, :, 0:QK_NOPE] = (acc[:, h * QK_NOPE:(h + 1) * QK_NOPE] * SM_SCALE).astype(BF16)
        o_ref[h, :, QK_NOPE:QK_DIM] = (rope[:, h * QK_ROPE:(h + 1) * QK_ROPE] * SM_SCALE).astype(BF16)


def q_projection(proj, g_norm, w, rope_c, rope_s, *, tm=512):
    g, l, _ = proj.shape
    tm = _pick(l, tm)
    tab = pl.BlockSpec((None, tm, ROPE_W), lambda b, i: (jnp.minimum(b, 1), i, 0))
    return pl.pallas_call(
        _qproj_kernel,
        out_shape=jax.ShapeDtypeStruct((g, MLA_HEADS, l, QK_DIM), BF16),
        grid=(g, l // tm),
        in_specs=[
            pl.BlockSpec((None, tm, Q_RANK), lambda b, i: (b, i, 0)),
            pl.BlockSpec((1, Q_RANK), lambda b, i: (0, 0)),
            pl.BlockSpec(w.shape, lambda b, i: (0, 0)),
            tab, tab,
        ],
        out_specs=pl.BlockSpec((None, MLA_HEADS, tm, QK_DIM), lambda b, i: (b, 0, i, 0)),
        compiler_params=_cparams(("parallel", "parallel")),
    )(proj, g_norm.reshape(1, Q_RANK), w, rope_c, rope_s)


def _kvproj_kernel(c_ref, kr_ref, w_ref, k_ref, v_ref):
    kv = jnp.dot(c_ref[...].astype(BF16), w_ref[...], preferred_element_type=F32)
    kr = kr_ref[...].astype(BF16)
    width = QK_NOPE + V_DIM
    for h in range(MLA_HEADS):
        k_ref[h, :, 0:QK_NOPE] = kv[:, h * width:h * width + QK_NOPE].astype(BF16)
        k_ref[h, :, QK_NOPE:QK_DIM] = kr
        v_ref[h] = kv[:, h * width + QK_NOPE:(h + 1) * width].astype(BF16)


def kv_projection(ckv, kr, w, *, tm=256):
    b, lk, _ = ckv.shape
    tm = _pick(lk, tm)
    return pl.pallas_call(
        _kvproj_kernel,
        out_shape=(jax.ShapeDtypeStruct((b, MLA_HEADS, lk, QK_DIM), BF16),
                   jax.ShapeDtypeStruct((b, MLA_HEADS, lk, V_DIM), BF16)),
        grid=(b, lk // tm),
        in_specs=[
            pl.BlockSpec((None, tm, KV_RANK), lambda bi, i: (bi, i, 0)),
            pl.BlockSpec((None, tm, QK_ROPE), lambda bi, i: (bi, i, 0)),
            pl.BlockSpec(w.shape, lambda bi, i: (0, 0)),
        ],
        out_specs=(pl.BlockSpec((None, MLA_HEADS, tm, QK_DIM), lambda bi, i: (bi, 0, i, 0)),
                   pl.BlockSpec((None, MLA_HEADS, tm, V_DIM), lambda bi, i: (bi, 0, i, 0))),
        compiler_params=_cparams(("parallel", "parallel")),
    )(ckv, kr, w)


def _attn_kernel(q_ref, k_ref, v_ref, o_ref):
    s = lax.dot_general(q_ref[...], k_ref[...], (((1,), (1,)), ((), ())),
                        preferred_element_type=F32)
    m = jnp.max(s, axis=-1, keepdims=True)
    p = jnp.exp(s - m)
    l = jnp.sum(p, axis=-1, keepdims=True)
    o = jnp.dot(p.astype(BF16), v_ref[...], preferred_element_type=F32)
    o_ref[...] = o / l


def attention(q, k, v, *, q_batch0=0, tq=256):
    b, h, lk, dq = k.shape
    l = q.shape[2]
    tq = _pick(l, tq)
    return pl.pallas_call(
        _attn_kernel,
        out_shape=jax.ShapeDtypeStruct((b, l, h * V_DIM), F32),
        grid=(b, h, l // tq),
        in_specs=[
            pl.BlockSpec((None, None, tq, dq), lambda bi, hi, i: (bi + q_batch0, hi, i, 0)),
            pl.BlockSpec((None, None, lk, dq), lambda bi, hi, i: (bi, hi, 0, 0)),
            pl.BlockSpec((None, None, lk, V_DIM), lambda bi, hi, i: (bi, hi, 0, 0)),
        ],
        out_specs=pl.BlockSpec((None, tq, V_DIM), lambda bi, hi, i: (bi, i, hi)),
        compiler_params=_cparams(("parallel", "parallel", "arbitrary")),
    )(q, k, v)


NEG_INF = float("-inf")


def _exchange(xs, i, j):
    xs[i], xs[j] = jnp.maximum(xs[i], xs[j]), jnp.minimum(xs[i], xs[j])


def _bitonic_merge_desc(xs):
    n = len(xs)
    j = n // 2
    while j >= 1:
        for i in range(n):
            if i & j == 0:
                _exchange(xs, i, i | j)
        j //= 2


def _sort_desc(xs):
    n = len(xs)
    k = 2
    while k <= n:
        j = k // 2
        while j >= 1:
            for i in range(n):
                l = i ^ j
                if l > i:
                    if i & k == 0:
                        _exchange(xs, i, l)
                    else:
                        _exchange(xs, l, i)
            j //= 2
        k *= 2


def _all_sublanes_max(m):
    for shift in (4, 2, 1):
        m = jnp.maximum(m, pltpu.roll(m, shift, 0))
    return m


def _top16_sorted(s):
    xs = [s[v * SUBLANES:(v + 1) * SUBLANES] for v in range(N_KEYS // SUBLANES)]
    _sort_desc(xs)
    for shift in (4, 2, 1):
        xs = [jnp.maximum(xs[r], pltpu.roll(xs[PEER_TOPK - 1 - r], shift, 0)) for r in range(PEER_TOPK)]
        _bitonic_merge_desc(xs)
    return xs


def _spread(rows, sub):
    out = rows[0]
    for b in range(1, SUBLANES):
        out = jnp.where(sub == b, rows[b], out)
    return out


def _all_sublanes_min(m):
    for shift in (4, 2, 1):
        m = jnp.minimum(m, pltpu.roll(m, shift, 0))
    return m


def _best_sums(cands, sub, count):
    ids = [sub + g * SUBLANES for g in range(len(cands))]
    far = len(cands) * SUBLANES
    best = []
    for _ in range(count):
        m = cands[0]
        for cnd in cands[1:]:
            m = jnp.maximum(m, cnd)
        m = _all_sublanes_max(m)
        best.append(m)
        first = jnp.where(cands[0] == m, ids[0], far)
        for cnd, idg in zip(cands[1:], ids[1:]):
            first = jnp.minimum(first, jnp.where(cnd == m, idg, far))
        first = _all_sublanes_min(first)
        cands = [jnp.where(idg == first, NEG_INF, cnd) for cnd, idg in zip(cands, ids)]
    return best


def _route_kernel(x_ref, sc_ref, sh_ref, wqt_ref, pk_ref, hb_ref, e1_ref, tau_ref, e2_ref, qt_ref, s_ref):
    tm = x_ref.shape[0]
    h = x_ref[...] * (1.0 + sc_ref[...]) + sh_ref[...]
    hb = h.astype(BF16)
    hb_ref[...] = hb
    qt = lax.dot_general(wqt_ref[...], hb, (((1,), (1,)), ((), ())), preferred_element_type=F32)
    for hp in range(2 * PEER_HEADS):
        qt_ref[hp] = qt[hp * PK_HALF:(hp + 1) * PK_HALF].astype(BF16)

    def head(hd, carry):
        for p in range(2):
            s_ref[p] = jnp.dot(pk_ref[2 * hd + p], qt_ref[2 * hd + p], preferred_element_type=F32)
        for lt in range(tm // LANES):
            ls = slice(lt * LANES, (lt + 1) * LANES)
            s1 = s_ref[0, :, ls]
            s2 = s_ref[1, :, ls]
            v1 = _top16_sorted(s1)
            v2 = _top16_sorted(s2)
            sub = lax.broadcasted_iota(jnp.int32, (SUBLANES, LANES), 0)
            v2_lo, v2_hi = _spread(v2[:SUBLANES], sub), _spread(v2[SUBLANES:], sub)
            v1_hi = _spread(v1[SUBLANES:], sub)
            cands = [v1[0] + v2_lo, v1[0] + v2_hi, v1_hi + v2[0]]
            cands += [v1[a] + v2_lo for a in range(1, SUBLANES)]
            best = _best_sums(cands, sub, PEER_TOPK + 1)
            z = jnp.ones_like(best[0])
            for k in range(1, PEER_TOPK):
                z = z + jnp.exp(best[k] - best[0])
            thr = 0.5 * (best[PEER_TOPK - 1] + best[PEER_TOPK])
            m1 = v1[0][0:1]
            m2 = v2[0][0:1]
            in1 = s1 >= v1[PEER_TOPK - 1][0:1]
            in2 = s2 >= v2[PEER_TOPK - 1][0:1]
            e1_ref[hd, :, ls] = jnp.exp(s1 - m1) / z[0:1]
            e2_ref[hd, :, ls] = jnp.where(in2, jnp.exp(s2 - m2), 0.0)
            tau_ref[hd, :, ls] = jnp.where(in1, jnp.exp((thr[0:1] - m2) - s1), jnp.inf)
        return carry

    lax.fori_loop(0, PEER_HEADS, head, 0)


def peer_route(x, scale, shift, wqt, pkeys, *, tm=256):
    g, l, d = x.shape
    t = g * l
    tm = _pick(l, tm)
    nl = l // tm
    fac = jax.ShapeDtypeStruct((PEER_HEADS, N_KEYS, t), F32)
    fspec = pl.BlockSpec((PEER_HEADS, N_KEYS, tm), lambda b, i: (0, 0, b * nl + i))
    return pl.pallas_call(
        _route_kernel,
        out_shape=(jax.ShapeDtypeStruct((g, l, d), BF16), fac, fac, fac),
        grid=(g, nl),
        in_specs=[
            pl.BlockSpec((None, tm, d), lambda b, i: (b, i, 0)),
            pl.BlockSpec((None, 1, d), lambda b, i: (b, 0, 0)),
            pl.BlockSpec((None, 1, d), lambda b, i: (b, 0, 0)),
            pl.BlockSpec(wqt.shape, lambda b, i: (0, 0)),
            pl.BlockSpec(pkeys.shape, lambda b, i: (0, 0, 0)),
        ],
        out_specs=(pl.BlockSpec((None, tm, d), lambda b, i: (b, i, 0)), fspec, fspec, fspec),
        scratch_shapes=[
            pltpu.VMEM((2 * PEER_HEADS, PK_HALF, tm), BF16),
            pltpu.VMEM((2, N_KEYS, tm), F32),
        ],
        compiler_params=_cparams(("parallel", "arbitrary")),
    )(x, scale, shift, wqt, pkeys)


GELU_C = math.sqrt(2.0 / math.pi)


def _gelu_tanh(a):
    half = 0.5 * a
    return half + half * jnp.tanh(a * (GELU_C + (GELU_C * 0.044715) * (a * a)))


EXPERT_CHUNK = SUBLANES * N_KEYS


def _experts_kernel(hb_ref, e1_ref, tau_ref, e2_ref, u_ref, vt_ref, x_ref, gate_ref, g_ref, b_ref,
                    o_ref, acc_ref, act_ref, p_ref, *, nc):
    c = pl.program_id(2)
    tm = hb_ref.shape[0]

    @pl.when(c == 0)
    def _():
        acc_ref[...] = jnp.zeros_like(acc_ref)

    act_ref[...] = lax.dot_general(u_ref[...], hb_ref[...], (((1,), (1,)), ((), ())),
                                   preferred_element_type=F32)
    for lt in range(tm // LANES):
        ls = slice(lt * LANES, (lt + 1) * LANES)
        for j in range(SUBLANES):
            w = jnp.zeros((N_KEYS, LANES), F32)
            for hd in range(PEER_HEADS):
                e2 = e2_ref[hd, :, ls]
                w = w + jnp.where(e2 >= tau_ref[hd, j:j + 1, ls], e2, 0.0) * e1_ref[hd, j:j + 1, ls]
            rs = slice(j * N_KEYS, (j + 1) * N_KEYS)
            p_ref[rs, ls] = (w * _gelu_tanh(act_ref[rs, ls])).astype(BF16)
    acc_ref[...] += jnp.dot(vt_ref[...], p_ref[...], preferred_element_type=F32)

    @pl.when(c == nc - 1)
    def _():
        y = ALPHA * x_ref[...] + gate_ref[...] * acc_ref[...].T
        o_ref[...] = _ln_rows(y, g_ref[...], b_ref[...])


def peer_experts(hb, e1, tau, e2, u, vt, x, gate, ln_g, ln_b, *, tm=512):
    g, l, d = x.shape
    tm = _pick(l, tm)
    nl = l // tm
    ec = EXPERT_CHUNK
    nc = N_EXPERTS // ec
    rspec = pl.BlockSpec((PEER_HEADS, SUBLANES, tm), lambda b, i, c: (0, c, b * nl + i))
    once = pl.Buffered(1)
    fspec = pl.BlockSpec((PEER_HEADS, N_KEYS, tm), lambda b, i, c: (0, 0, b * nl + i), pipeline_mode=once)
    tok_in = pl.BlockSpec((None, tm, d), lambda b, i, c: (b, i, 0), pipeline_mode=once)
    tok = pl.BlockSpec((None, tm, d), lambda b, i, c: (b, i, 0))
    vec = pl.BlockSpec((1, d), lambda b, i, c: (0, 0))
    return pl.pallas_call(
        functools.partial(_experts_kernel, nc=nc),
        out_shape=jax.ShapeDtypeStruct((g, l, d), F32),
        grid=(g, nl, nc),
        in_specs=[
            tok_in, rspec, rspec, fspec,
            pl.BlockSpec((ec, d), lambda b, i, c: (c, 0)),
            pl.BlockSpec((d, ec), lambda b, i, c: (0, c)),
            tok_in, pl.BlockSpec((None, 1, d), lambda b, i, c: (b, 0, 0)), vec, vec,
        ],
        out_specs=tok,
        scratch_shapes=[
            pltpu.VMEM((d, tm), F32),
            pltpu.VMEM((ec, tm), F32),
            pltpu.VMEM((ec, tm), BF16),
        ],
        compiler_params=_cparams(("parallel", "parallel", "arbitrary")),
    )(hb, e1, tau, e2, u, vt, x, gate, ln_g.reshape(1, d), ln_b.reshape(1, d))


def _rms_norm(x, g):
    return x * lax.rsqrt(jnp.mean(jnp.square(x), axis=-1, keepdims=True) + RMS_EPS) * g


def _rope_tables(l):
    rows = l // GRID_W
    row = jnp.repeat(jnp.arange(rows), GRID_W)
    col = jnp.tile(jnp.arange(GRID_W), rows)
    pos = jnp.stack([row, col], axis=-1).astype(F32)
    inv = ROPE_THETA ** (-jnp.arange(ROPE_PAIRS, dtype=F32) / ROPE_PAIRS)
    ang = pos[:, :, None] * inv
    return jnp.cos(ang), jnp.sin(ang)


def _rope_lane_tables(l):
    cos, sin = _rope_tables(l)
    c = jnp.broadcast_to(cos[:, :, None, :], (l, 2, 2, ROPE_PAIRS)).reshape(l, QK_ROPE)
    s = jnp.stack([-sin, sin], axis=2).reshape(l, QK_ROPE)
    c = jnp.tile(c, (1, MLA_HEADS))
    s = jnp.tile(s, (1, MLA_HEADS))
    return jnp.stack([jnp.ones_like(c), c]), jnp.stack([jnp.zeros_like(s), s])


def _extend_w_uq(w_uq):
    depth = w_uq.shape[0]
    w4 = w_uq.reshape(depth, Q_RANK, MLA_HEADS, QK_DIM)
    nope = w4[..., :QK_NOPE].reshape(depth, Q_RANK, NOPE_W)
    rope = w4[..., QK_NOPE:]
    swapped = jnp.flip(rope.reshape(depth, Q_RANK, MLA_HEADS, 2, 2, ROPE_PAIRS), axis=-2)
    return jnp.concatenate([nope, rope.reshape(depth, Q_RANK, ROPE_W),
                            swapped.reshape(depth, Q_RANK, ROPE_W)], axis=-1).astype(BF16)


def _apply_rope(x, cos, sin):
    b, l, h, _ = x.shape
    xr = x.reshape(b, l, h, 2, 2, ROPE_PAIRS)
    x1, x2 = xr[..., 0, :], xr[..., 1, :]
    c = cos[None, :, None]
    s = sin[None, :, None]
    out = jnp.stack([x1 * c - x2 * s, x2 * c + x1 * s], axis=-2)
    return out.reshape(b, l, h, QK_ROPE)


def _conv3(x, w):
    xp = jnp.pad(x, ((0, 0), (1, 1), (0, 0)))
    return xp[:, :-2] * w[0] + xp[:, 1:-1] * w[1] + xp[:, 2:] * w[2]


def _dft_matrices(l):
    n = 2 * l
    k = jnp.arange(l, dtype=jnp.int32)[:, None]
    t = jnp.arange(n, dtype=jnp.int32)[None, :]
    step = 64
    t_hi = jnp.arange(n // step, dtype=jnp.int32)[None, :] * step
    t_lo = jnp.arange(step, dtype=jnp.int32)[None, :]
    a_hi = ((k * t_hi) % n).astype(F32) * (2.0 * math.pi / n)
    a_lo = ((k * t_lo) % n).astype(F32) * (2.0 * math.pi / n)
    ch, sh = jnp.cos(a_hi)[:, :, None], jnp.sin(a_hi)[:, :, None]
    cl, sl = jnp.cos(a_lo)[:, None, :], jnp.sin(a_lo)[:, None, :]
    cos = (ch * cl - sh * sl).reshape(l, n)
    msin = -(sh * cl + ch * sl).reshape(l, n)
    nyq = jnp.where(t % 2 == 0, 1.0, -1.0).astype(F32)
    msin = jnp.where(k == 0, nyq, msin)
    fwd = jnp.concatenate([cos, msin], axis=0)
    wgt = jnp.where(k == 0, 1.0 / n, 2.0 / n)
    inv = jnp.concatenate([cos[:, :l] * wgt, msin[:, :l] * wgt], axis=0).T
    fwd = fwd.astype(BF16)
    return fwd, fwd[:, :l], inv.astype(BF16)


def _hyena_filter_spectrum(l, p, fwd):
    t = jnp.linspace(0.0, 1.0, l, dtype=F32)[:, None]
    bands = (HY_EMB - 1) // 2
    w = 2.0 * math.pi * jnp.arange(l, dtype=F32) / l
    fr_b = jnp.linspace(1e-4, bands - 1, bands, dtype=F32)
    ang = w[:, None] * fr_b[None]
    z = jnp.concatenate([t, jnp.cos(ang), -jnp.sin(ang)], axis=-1)
    zp = jnp.pad(z, ((0, 0), (0, LANES - HY_EMB)))
    w1 = jnp.pad(p['hy_f_w1'], ((0, LANES - HY_EMB), (0, LANES - HY_HIDDEN)))
    w2 = jnp.pad(p['hy_f_w2'], ((0, LANES - HY_HIDDEN), (0, LANES - HY_HIDDEN)))
    w3 = jnp.pad(p['hy_f_w3'], ((0, LANES - HY_HIDDEN), (0, 0)))
    pad_h = (0, LANES - HY_HIDDEN)
    fr = jnp.pad(p['hy_f_freq'], pad_h)
    hid = jnp.sin(fr * (matmul_f32(zp, w1) + jnp.pad(p['hy_f_b1'], pad_h)))
    hid = jnp.sin(fr * (matmul_f32(hid, w2) + jnp.pad(p['hy_f_b2'], pad_h)))
    h = matmul_f32(hid, w3).reshape(l, 2, HY_ORDER, HY_W)
    h = h * jnp.exp(-t[:, :, None, None] * jnp.abs(p['hy_decay']))
    k = jnp.concatenate([h[:, 0], jnp.zeros((1, HY_ORDER, HY_W), F32), h[1:, 1][::-1]], axis=0)
    k = k / jnp.sum(jnp.abs(k), axis=0, keepdims=True)
    spec = matmul(fwd, k.reshape(2 * l, HY_ORDER * HY_W))
    kr, ki = spec[:l], spec[l:]
    first = (jnp.arange(l) == 0)[:, None]
    a = kr
    b = jnp.where(first, 0.0, ki)
    d = jnp.where(first, ki, kr)
    return a, b, d


def _long_conv(z, coef, fwd_half, inv, o):
    l = z.shape[1]
    a, b, d = (c[:, o * HY_W:(o + 1) * HY_W] for c in coef)
    spec = matmul(fwd_half, z.astype(BF16), tm=1024)
    xr, xi = spec[:, :l], spec[:, l:]
    y = jnp.concatenate([xr * a - xi * b, xr * b + xi * d], axis=1).astype(BF16)
    return matmul(inv, y, tm=1024)


def _hyena(u_in, p, dft, coef):
    _, fwd_half, inv = dft
    u = _conv3(u_in, p['hy_sconv_w']) + p['hy_sconv_b']
    v, x1, x2 = jnp.split(u, 3, axis=-1)
    z = v
    for o, gate in enumerate((x1, x2)):
        z = gate * (_long_conv(z, coef, fwd_half, inv, o) + p['hy_bias'][o] * z)
    return z


def _mixers(proj, qf, q_batch0, p, dft, coef, ctx_ckv, ctx_krope, rope):
    kv_lat = proj[..., Q_RANK:KR_OFF]
    k_r = proj[..., KR_OFF:KR_OFF + QK_ROPE]
    sc_in = proj[..., SC_OFF:HY_OFF]
    hy_in = proj[..., HY_OFF:]
    ckv = _rms_norm(kv_lat, p['kv_norm'])
    if rope is None:
        ckv_all, kr_all = ckv, k_r
    else:
        cos, sin = rope
        kr_rot = _apply_rope(k_r[:, :, None], cos, sin)[:, :, 0]
        ckv_all = jnp.concatenate([ctx_ckv, ckv], axis=1)
        kr_all = jnp.concatenate([ctx_krope, kr_rot], axis=1)
    kf, vf = kv_projection(ckv_all, kr_all, p['w_ukv'])
    att = attention(qf, kf, vf, q_batch0=q_batch0)
    bg, cg, hs = jnp.split(sc_in, 3, axis=-1)
    sc = bg * _conv3(cg * hs, p['sc_conv'])
    hy = _hyena(hy_in, p, dft, coef)
    return jnp.concatenate([att, sc, hy], axis=-1), ckv, k_r


def _layer(x_all, p, cache_ckv_l, cache_krope_l, dfts, rope, n_ctx, ctx_len):
    g, lg, d = x_all.shape
    mod = p['mod']
    proj = in_projection(x_all, mod[:, 1:2], mod[:, 0:1], p['w_in'])
    coef_c = _hyena_filter_spectrum(ctx_len, p, dfts[0][0])
    coef_s = _hyena_filter_spectrum(lg, p, dfts[1][0])
    qf = q_projection(proj, p['q_norm'], p['w_uq'], rope[2], rope[3])
    q_ctx = qf[0].reshape(MLA_HEADS, n_ctx, ctx_len, QK_DIM).transpose(1, 0, 2, 3)
    cat_c, ckv, k_r = _mixers(proj[0].reshape(n_ctx, ctx_len, -1), q_ctx, 0, p, dfts[0], coef_c,
                              None, None, None)
    cat_s, _, _ = _mixers(proj[1:], qf, 1, p, dfts[1], coef_s, cache_ckv_l, cache_krope_l, rope[:2])
    cat = jnp.concatenate([cat_c.reshape(1, lg, d), cat_s], axis=0)
    x1 = out_projection(cat, x_all, mod[:, 2:3], p['w_out'], p['ln1_g'], p['ln1_b'])
    hb, e1, tau, e2 = peer_route(x1, mod[:, 4:5], mod[:, 3:4], p['wqt'], p['pkeys'])
    x2 = peer_experts(hb, e1, tau, e2, p['peer_u'], p['peer_vt'], x1, mod[:, 5:6], p['ln2_g'], p['ln2_b'])
    return x2, ckv, k_r


def kernel(x_prompt, x_sample, cache_ckv, cache_krope, c, c_ctx, w_mod, b_mod, w_in, mla_q_norm, mla_kv_norm, w_uq, w_ukv, sc_conv, hy_sconv_w, hy_sconv_b, hy_f_w1, hy_f_b1, hy_f_w2, hy_f_b2, hy_f_w3, hy_f_freq, hy_decay, hy_bias, w_out, ln1_g, ln1_b, ln2_g, ln2_b, pk_query, pk_keys, peer_u, peer_v):
    n_ctx, ctx_len, d = x_prompt.shape
    n_lat, lat_len, _ = x_sample.shape
    depth = w_in.shape[0]
    assert n_ctx * ctx_len == lat_len
    g = n_lat + 1

    cond = jnp.concatenate([c_ctx[None], c], axis=0)
    cond = jnp.pad(jax.nn.silu(cond), ((0, 16 - g), (0, 0)))
    mod = matmul(cond, w_mod, tm=16, tn=1536)[:, :g] + b_mod[:, None]
    mod = mod.reshape(depth, g, N_MOD, d)

    w_in_p = jnp.concatenate(
        [w_in[..., :KR_OFF + QK_ROPE],
         jnp.zeros((depth, d, SC_OFF - KR_OFF - QK_ROPE), w_in.dtype),
         w_in[..., KR_OFF + QK_ROPE:]], axis=-1).astype(BF16)
    layers = {
        'mod': mod, 'w_in': w_in_p, 'q_norm': mla_q_norm, 'kv_norm': mla_kv_norm,
        'w_uq': _extend_w_uq(w_uq), 'w_ukv': w_ukv.astype(BF16), 'sc_conv': sc_conv,
        'hy_sconv_w': hy_sconv_w, 'hy_sconv_b': hy_sconv_b, 'hy_f_w1': hy_f_w1, 'hy_f_b1': hy_f_b1,
        'hy_f_w2': hy_f_w2, 'hy_f_b2': hy_f_b2, 'hy_f_w3': hy_f_w3, 'hy_f_freq': hy_f_freq,
        'hy_decay': hy_decay, 'hy_bias': hy_bias, 'w_out': w_out.astype(BF16),
        'ln1_g': ln1_g, 'ln1_b': ln1_b, 'ln2_g': ln2_g, 'ln2_b': ln2_b,
        'wqt': jnp.swapaxes(pk_query, 1, 2).astype(BF16),
        'pkeys': pk_keys.reshape(depth, 2 * PEER_HEADS, N_KEYS, PK_HALF).astype(BF16),
        'peer_u': peer_u.astype(BF16),
        'peer_vt': jnp.swapaxes(peer_v.astype(BF16), 1, 2),
        'cache_ckv': jnp.swapaxes(cache_ckv, 0, 1),
        'cache_krope': jnp.swapaxes(cache_krope, 0, 1),
    }
    dfts = (_dft_matrices(ctx_len), _dft_matrices(lat_len))
    rope = _rope_tables(lat_len) + _rope_lane_tables(lat_len)

    def step(x_all, p):
        x_new, ckv, k_r = _layer(x_all, p, p['cache_ckv'], p['cache_krope'], dfts, rope, n_ctx, ctx_len)
        return x_new, (ckv, k_r)

    x_all = jnp.concatenate([x_prompt.reshape(1, lat_len, d), x_sample], axis=0)
    x_all, (ckv, k_r) = lax.scan(step, x_all, layers)
    y_prompt = x_all[0].reshape(n_ctx, ctx_len, d)
    y_sample = x_all[1:]
    return y_prompt, y_sample, jnp.swapaxes(ckv, 0, 1), jnp.swapaxes(k_r, 0, 1)
```

```python
import functools
import math

import jax
import jax.numpy as jnp
from jax import lax
from jax.experimental import pallas as pl
from jax.experimental.pallas import tpu as pltpu

F32 = jnp.float32
BF16 = jnp.bfloat16

D_MODEL = 2048
DEPTH = 4
GRID_W = 64
MLA_HEADS = 8
QK_NOPE = 128
QK_ROPE = 64
QK_DIM = QK_NOPE + QK_ROPE
V_DIM = 128
Q_RANK = D_MODEL // 4
KV_RANK = D_MODEL // 8
ROPE_PAIRS = QK_ROPE // 4
ROPE_THETA = 10000.0
SC_W = D_MODEL // 4
HY_W = D_MODEL // 4
HY_ORDER = 2
HY_EMB = 33
HY_HIDDEN = 64
PEER_HEADS = 8
N_KEYS = 128
N_EXPERTS = N_KEYS * N_KEYS
PK_HALF = 128
PEER_TOPK = 16
N_MOD = 6
ALPHA = (2 * DEPTH) ** 0.25
SM_SCALE = QK_DIM ** -0.5
LN_EPS = 1e-5
RMS_EPS = 1e-6

KR_OFF = Q_RANK + KV_RANK
SC_OFF = 1024
HY_OFF = SC_OFF + 3 * SC_W
IN_COLS_PAD = HY_OFF + 3 * HY_W

VMEM_LIMIT = 52 * 1024 * 1024
LANES = 128
SUBLANES = 8


def _cparams(sem):
    return pltpu.CompilerParams(dimension_semantics=sem, vmem_limit_bytes=VMEM_LIMIT)


def _mm_kernel(a_ref, b_ref, o_ref, acc_ref, *, nk):
    part = jnp.dot(a_ref[...].astype(BF16), b_ref[...].astype(BF16), preferred_element_type=F32)
    if nk == 1:
        o_ref[...] = part.astype(o_ref.dtype)
        return
    k = pl.program_id(3)

    @pl.when(k == 0)
    def _():
        acc_ref[...] = part

    @pl.when(k > 0)
    def _():
        acc_ref[...] += part

    @pl.when(k == nk - 1)
    def _():
        o_ref[...] = acc_ref[...].astype(o_ref.dtype)


def _pick(n, pref):
    t = min(n, pref)
    while n % t:
        t //= 2
    return t


def matmul(a, b, *, tm=512, tn=512, tk=2048, out_dtype=F32):
    squeeze = a.ndim == 2 and b.ndim == 2
    if a.ndim == 2:
        a = a[None]
    if b.ndim == 2:
        b = b[None]
    ba, m, k = a.shape
    bb, k2, n = b.shape
    assert k == k2 and (ba == bb or 1 in (ba, bb))
    nb = max(ba, bb)
    tm, tn, tk = _pick(m, tm), _pick(n, tn), _pick(k, tk)
    nk = k // tk
    out = pl.pallas_call(
        functools.partial(_mm_kernel, nk=nk),
        out_shape=jax.ShapeDtypeStruct((nb, m, n), out_dtype),
        grid=(nb, m // tm, n // tn, nk),
        in_specs=[
            pl.BlockSpec((None, tm, tk), lambda g, i, j, kk: (g if ba > 1 else 0, i, kk)),
            pl.BlockSpec((None, tk, tn), lambda g, i, j, kk: (g if bb > 1 else 0, kk, j)),
        ],
        out_specs=pl.BlockSpec((None, tm, tn), lambda g, i, j, kk: (g, i, j)),
        scratch_shapes=[pltpu.VMEM((tm, tn), F32)],
        compiler_params=_cparams(("parallel", "parallel", "parallel", "arbitrary")),
    )(a, b)
    return out[0] if squeeze else out


def _mm_f32_kernel(a_ref, b_ref, o_ref):
    o_ref[...] = jnp.dot(a_ref[...], b_ref[...], preferred_element_type=F32,
                         precision=lax.Precision.HIGHEST)


def matmul_f32(a, b, *, tm=512):
    m, k = a.shape
    n = b.shape[1]
    tm = _pick(m, tm)
    return pl.pallas_call(
        _mm_f32_kernel,
        out_shape=jax.ShapeDtypeStruct((m, n), F32),
        grid=(m // tm,),
        in_specs=[pl.BlockSpec((tm, k), lambda i: (i, 0)), pl.BlockSpec((k, n), lambda i: (0, 0))],
        out_specs=pl.BlockSpec((tm, n), lambda i: (i, 0)),
        compiler_params=_cparams(("parallel",)),
    )(a, b)


def _inproj_kernel(x_ref, sc_ref, sh_ref, w_ref, o_ref):
    h = x_ref[...] * (1.0 + sc_ref[...]) + sh_ref[...]
    o_ref[...] = jnp.dot(h.astype(BF16), w_ref[...], preferred_element_type=F32)


def in_projection(x, scale, shift, w, *, tm=512, tn=2048):
    g, l, d = x.shape
    n = w.shape[1]
    tm, tn = _pick(l, tm), _pick(n, tn)
    return pl.pallas_call(
        _inproj_kernel,
        out_shape=jax.ShapeDtypeStruct((g, l, n), F32),
        grid=(g, l // tm, n // tn),
        in_specs=[
            pl.BlockSpec((None, tm, d), lambda b, i, j: (b, i, 0)),
            pl.BlockSpec((None, 1, d), lambda b, i, j: (b, 0, 0)),
            pl.BlockSpec((None, 1, d), lambda b, i, j: (b, 0, 0)),
            pl.BlockSpec((d, tn), lambda b, i, j: (0, j)),
        ],
        out_specs=pl.BlockSpec((None, tm, tn), lambda b, i, j: (b, i, j)),
        compiler_params=_cparams(("parallel", "parallel", "arbitrary")),
    )(x, scale, shift, w)


def _ln_rows(y, g, b):
    mu = jnp.mean(y, axis=-1, keepdims=True)
    yc = y - mu
    var = jnp.mean(yc * yc, axis=-1, keepdims=True)
    return yc * lax.rsqrt(var + LN_EPS) * g + b


def _outproj_kernel(cat_ref, x_ref, gate_ref, w_ref, g_ref, b_ref, o_ref):
    mix = jnp.dot(cat_ref[...].astype(BF16), w_ref[...], preferred_element_type=F32)
    o_ref[...] = _ln_rows(ALPHA * x_ref[...] + gate_ref[...] * mix, g_ref[...], b_ref[...])


def out_projection(cat, x, gate, w, ln_g, ln_b, *, tm=512):
    g, l, d = x.shape
    tm = _pick(l, tm)
    tok = pl.BlockSpec((None, tm, d), lambda b, i: (b, i, 0))
    vec = pl.BlockSpec((1, d), lambda b, i: (0, 0))
    return pl.pallas_call(
        _outproj_kernel,
        out_shape=jax.ShapeDtypeStruct((g, l, d), F32),
        grid=(g, l // tm),
        in_specs=[tok, tok, pl.BlockSpec((None, 1, d), lambda b, i: (b, 0, 0)),
                  pl.BlockSpec((d, d), lambda b, i: (0, 0)), vec, vec],
        out_specs=tok,
        compiler_params=_cparams(("parallel", "parallel")),
    )(cat, x, gate, w, ln_g.reshape(1, d), ln_b.reshape(1, d))


def _conv3_rows(u, w):
    l = u.shape[0]
    row = lax.broadcasted_iota(jnp.int32, u.shape, 0)
    prev = jnp.where(row == 0, 0.0, pltpu.roll(u, 1, 0))
    nxt = jnp.where(row == l - 1, 0.0, pltpu.roll(u, l - 1, 0))
    return prev * w[0:1] + u * w[1:2] + nxt * w[2:3]


def _gated_conv_kernel(bg_ref, cg_ref, hs_ref, w_ref, o_ref):
    o_ref[...] = bg_ref[...] * _conv3_rows(cg_ref[...] * hs_ref[...], w_ref[...])


def _bias_conv_kernel(x_ref, w_ref, b_ref, o_ref):
    o_ref[...] = _conv3_rows(x_ref[...], w_ref[...]) + b_ref[...]


def _col_spec(l, batch0, col0):
    return pl.BlockSpec((None, l, LANES), lambda b, j: (b + batch0, 0, col0 + j))


def short_gated_conv(proj, w, nb, batch0):
    l = proj.shape[1]
    c0 = SC_OFF // LANES
    nj = SC_W // LANES
    return pl.pallas_call(
        _gated_conv_kernel,
        out_shape=jax.ShapeDtypeStruct((nb, l, SC_W), F32),
        grid=(nb, nj),
        in_specs=[_col_spec(l, batch0, c0), _col_spec(l, batch0, c0 + nj), _col_spec(l, batch0, c0 + 2 * nj),
                  pl.BlockSpec((3, LANES), lambda b, j: (0, j))],
        out_specs=pl.BlockSpec((None, l, LANES), lambda b, j: (b, 0, j)),
        compiler_params=_cparams(("parallel", "parallel")),
    )(proj, proj, proj, w)


def hyena_input_conv(proj, w, bias, nb, batch0):
    l = proj.shape[1]
    width = 3 * HY_W
    return pl.pallas_call(
        _bias_conv_kernel,
        out_shape=jax.ShapeDtypeStruct((nb, l, width), F32),
        grid=(nb, width // LANES),
        in_specs=[_col_spec(l, batch0, HY_OFF // LANES),
                  pl.BlockSpec((3, LANES), lambda b, j: (0, j)),
                  pl.BlockSpec((1, LANES), lambda b, j: (0, j))],
        out_specs=pl.BlockSpec((None, l, LANES), lambda b, j: (b, 0, j)),
        compiler_params=_cparams(("parallel", "parallel")),
    )(proj, w, bias.reshape(1, width))


NOPE_W = MLA_HEADS * QK_NOPE
ROPE_W = MLA_HEADS * QK_ROPE


def _qproj_kernel(p_ref, g_ref, w_ref, c_ref, s_ref, o_ref):
    x = p_ref[...]
    xn = x * lax.rsqrt(jnp.mean(x * x, axis=-1, keepdims=True) + RMS_EPS) * g_ref[...]
    acc = jnp.dot(xn.astype(BF16), w_ref[...], preferred_element_type=F32)
    rope = acc[:, NOPE_W:NOPE_W + ROPE_W] * c_ref[...] + acc[:, NOPE_W + ROPE_W:] * s_ref[...]
    for h in range(MLA_HEADS):
        o_ref[h, :, 0:QK_NOPE] = (acc[:, h * QK_NOPE:(h + 1) * QK_NOPE] * SM_SCALE).astype(BF16)
        o_ref[h, :, QK_NOPE:QK_DIM] = (rope[:, h * QK_ROPE:(h + 1) * QK_ROPE] * SM_SCALE).astype(BF16)


def q_projection(proj, g_norm, w, rope_c, rope_s, *, tm=512):
    g, l, _ = proj.shape
    tm = _pick(l, tm)
    tab = pl.BlockSpec((None, tm, ROPE_W), lambda b, i: (jnp.minimum(b, 1), i, 0))
    return pl.pallas_call(
        _qproj_kernel,
        out_shape=jax.ShapeDtypeStruct((g, MLA_HEADS, l, QK_DIM), BF16),
        grid=(g, l // tm),
        in_specs=[
            pl.BlockSpec((None, tm, Q_RANK), lambda b, i: (b, i, 0)),
            pl.BlockSpec((1, Q_RANK), lambda b, i: (0, 0)),
            pl.BlockSpec(w.shape, lambda b, i: (0, 0)),
            tab, tab,
        ],
        out_specs=pl.BlockSpec((None, MLA_HEADS, tm, QK_DIM), lambda b, i: (b, 0, i, 0)),
        compiler_params=_cparams(("parallel", "parallel")),
    )(proj, g_norm.reshape(1, Q_RANK), w, rope_c, rope_s)


def _kvproj_kernel(c_ref, kr_ref, w_ref, k_ref, v_ref):
    kv = jnp.dot(c_ref[...].astype(BF16), w_ref[...], preferred_element_type=F32)
    kr = kr_ref[...].astype(BF16)
    width = QK_NOPE + V_DIM
    for h in range(MLA_HEADS):
        k_ref[h, :, 0:QK_NOPE] = kv[:, h * width:h * width + QK_NOPE].astype(BF16)
        k_ref[h, :, QK_NOPE:QK_DIM] = kr
        v_ref[h] = kv[:, h * width + QK_NOPE:(h + 1) * width].astype(BF16)


def kv_projection(ckv, kr, w, *, tm=256):
    b, lk, _ = ckv.shape
    tm = _pick(lk, tm)
    return pl.pallas_call(
        _kvproj_kernel,
        out_shape=(jax.ShapeDtypeStruct((b, MLA_HEADS, lk, QK_DIM), BF16),
                   jax.ShapeDtypeStruct((b, MLA_HEADS, lk, V_DIM), BF16)),
        grid=(b, lk // tm),
        in_specs=[
            pl.BlockSpec((None, tm, KV_RANK), lambda bi, i: (bi, i, 0)),
            pl.BlockSpec((None, tm, QK_ROPE), lambda bi, i: (bi, i, 0)),
            pl.BlockSpec(w.shape, lambda bi, i: (0, 0)),
        ],
        out_specs=(pl.BlockSpec((None, MLA_HEADS, tm, QK_DIM), lambda bi, i: (bi, 0, i, 0)),
                   pl.BlockSpec((None, MLA_HEADS, tm, V_DIM), lambda bi, i: (bi, 0, i, 0))),
        compiler_params=_cparams(("parallel", "parallel")),
    )(ckv, kr, w)


def _attn_kernel(q_ref, k_ref, v_ref, o_ref):
    s = lax.dot_general(q_ref[...], k_ref[...], (((1,), (1,)), ((), ())),
                        preferred_element_type=F32)
    m = jnp.max(s, axis=-1, keepdims=True)
    p = jnp.exp(s - m)
    l = jnp.sum(p, axis=-1, keepdims=True)
    o = jnp.dot(p.astype(BF16), v_ref[...], preferred_element_type=F32)
    o_ref[...] = o / l


def attention(q, k, v, *, q_batch0=0, tq=512):
    b, h, lk, dq = k.shape
    l = q.shape[2]
    tq = _pick(l, tq)
    return pl.pallas_call(
        _attn_kernel,
        out_shape=jax.ShapeDtypeStruct((b, l, h * V_DIM), F32),
        grid=(b, h, l // tq),
        in_specs=[
            pl.BlockSpec((None, None, tq, dq), lambda bi, hi, i: (bi + q_batch0, hi, i, 0)),
            pl.BlockSpec((None, None, lk, dq), lambda bi, hi, i: (bi, hi, 0, 0)),
            pl.BlockSpec((None, None, lk, V_DIM), lambda bi, hi, i: (bi, hi, 0, 0)),
        ],
        out_specs=pl.BlockSpec((None, tq, V_DIM), lambda bi, hi, i: (bi, i, hi)),
        compiler_params=_cparams(("parallel", "parallel", "arbitrary")),
    )(q, k, v)


NEG_INF = float("-inf")


def _exchange(xs, i, j):
    xs[i], xs[j] = jnp.maximum(xs[i], xs[j]), jnp.minimum(xs[i], xs[j])


def _bitonic_merge_desc(xs):
    n = len(xs)
    j = n // 2
    while j >= 1:
        for i in range(n):
            if i & j == 0:
                _exchange(xs, i, i | j)
        j //= 2


def _sort_desc(xs):
    n = len(xs)
    k = 2
    while k <= n:
        j = k // 2
        while j >= 1:
            for i in range(n):
                l = i ^ j
                if l > i:
                    if i & k == 0:
                        _exchange(xs, i, l)
                    else:
                        _exchange(xs, l, i)
            j //= 2
        k *= 2


def _all_sublanes_max(m):
    for shift in (4, 2, 1):
        m = jnp.maximum(m, pltpu.roll(m, shift, 0))
    return m


def _top16_sorted(s):
    xs = [s[v * SUBLANES:(v + 1) * SUBLANES] for v in range(N_KEYS // SUBLANES)]
    _sort_desc(xs)
    for shift in (4, 2, 1):
        xs = [jnp.maximum(xs[r], pltpu.roll(xs[PEER_TOPK - 1 - r], shift, 0)) for r in range(PEER_TOPK)]
        _bitonic_merge_desc(xs)
    return xs


def _spread(rows, sub):
    out = rows[0]
    for b in range(1, SUBLANES):
        out = jnp.where(sub == b, rows[b], out)
    return out


def _all_sublanes_min(m):
    for shift in (4, 2, 1):
        m = jnp.minimum(m, pltpu.roll(m, shift, 0))
    return m


def _best_sums(cands, sub, count):
    ids = [sub + g * SUBLANES for g in range(len(cands))]
    far = len(cands) * SUBLANES
    best = []
    for _ in range(count):
        m = cands[0]
        for cnd in cands[1:]:
            m = jnp.maximum(m, cnd)
        m = _all_sublanes_max(m)
        best.append(m)
        first = jnp.where(cands[0] == m, ids[0], far)
        for cnd, idg in zip(cands[1:], ids[1:]):
            first = jnp.minimum(first, jnp.where(cnd == m, idg, far))
        first = _all_sublanes_min(first)
        cands = [jnp.where(idg == first, NEG_INF, cnd) for cnd, idg in zip(cands, ids)]
    return best


def _route_kernel(x_ref, sc_ref, sh_ref, wqt_ref, pk_ref, hb_ref, e1_ref, tau_ref, e2_ref, qt_ref, s_ref):
    tm = x_ref.shape[0]
    h = x_ref[...] * (1.0 + sc_ref[...]) + sh_ref[...]
    hb = h.astype(BF16)
    hb_ref[...] = hb
    qt = lax.dot_general(wqt_ref[...], hb, (((1,), (1,)), ((), ())), preferred_element_type=F32)
    for hp in range(2 * PEER_HEADS):
        qt_ref[hp] = qt[hp * PK_HALF:(hp + 1) * PK_HALF].astype(BF16)

    def head(hd, carry):
        for p in range(2):
            s_ref[p] = jnp.dot(pk_ref[2 * hd + p], qt_ref[2 * hd + p], preferred_element_type=F32)
        for lt in range(tm // LANES):
            ls = slice(lt * LANES, (lt + 1) * LANES)
            s1 = s_ref[0, :, ls]
            s2 = s_ref[1, :, ls]
            v1 = _top16_sorted(s1)
            v2 = _top16_sorted(s2)
            sub = lax.broadcasted_iota(jnp.int32, (SUBLANES, LANES), 0)
            v2_lo, v2_hi = _spread(v2[:SUBLANES], sub), _spread(v2[SUBLANES:], sub)
            v1_hi = _spread(v1[SUBLANES:], sub)
            cands = [v1[0] + v2_lo, v1[0] + v2_hi, v1_hi + v2[0]]
            cands += [v1[a] + v2_lo for a in range(1, SUBLANES)]
            best = _best_sums(cands, sub, PEER_TOPK + 1)
            z = jnp.ones_like(best[0])
            for k in range(1, PEER_TOPK):
                z = z + jnp.exp(best[k] - best[0])
            thr = 0.5 * (best[PEER_TOPK - 1] + best[PEER_TOPK])
            m1 = v1[0][0:1]
            m2 = v2[0][0:1]
            in1 = s1 >= v1[PEER_TOPK - 1][0:1]
            in2 = s2 >= v2[PEER_TOPK - 1][0:1]
            e1_ref[hd, :, ls] = jnp.exp(s1 - m1) / z[0:1]
            e2_ref[hd, :, ls] = jnp.where(in2, jnp.exp(s2 - m2), 0.0)
            tau_ref[hd, :, ls] = jnp.where(in1, jnp.exp((thr[0:1] - m2) - s1), jnp.inf)
        return carry

    lax.fori_loop(0, PEER_HEADS, head, 0)


def peer_route(x, scale, shift, wqt, pkeys, *, tm=256):
    g, l, d = x.shape
    t = g * l
    tm = _pick(l, tm)
    nl = l // tm
    fac = jax.ShapeDtypeStruct((PEER_HEADS, N_KEYS, t), F32)
    fspec = pl.BlockSpec((PEER_HEADS, N_KEYS, tm), lambda b, i: (0, 0, b * nl + i))
    return pl.pallas_call(
        _route_kernel,
        out_shape=(jax.ShapeDtypeStruct((g, l, d), BF16), fac, fac, fac),
        grid=(g, nl),
        in_specs=[
            pl.BlockSpec((None, tm, d), lambda b, i: (b, i, 0)),
            pl.BlockSpec((None, 1, d), lambda b, i: (b, 0, 0)),
            pl.BlockSpec((None, 1, d), lambda b, i: (b, 0, 0)),
            pl.BlockSpec(wqt.shape, lambda b, i: (0, 0)),
            pl.BlockSpec(pkeys.shape, lambda b, i: (0, 0, 0)),
        ],
        out_specs=(pl.BlockSpec((None, tm, d), lambda b, i: (b, i, 0)), fspec, fspec, fspec),
        scratch_shapes=[
            pltpu.VMEM((2 * PEER_HEADS, PK_HALF, tm), BF16),
            pltpu.VMEM((2, N_KEYS, tm), F32),
        ],
        compiler_params=_cparams(("parallel", "arbitrary")),
    )(x, scale, shift, wqt, pkeys)


GELU_C = math.sqrt(2.0 / math.pi)


def _gelu_tanh(a):
    half = 0.5 * a
    return half + half * jnp.tanh(a * (GELU_C + (GELU_C * 0.044715) * (a * a)))


EXPERT_CHUNK = SUBLANES * N_KEYS


def _experts_kernel(hb_ref, e1_ref, tau_ref, e2_ref, u_ref, vt_ref, x_ref, gate_ref, g_ref, b_ref,
                    o_ref, acc_ref, act_ref, p_ref, *, nc):
    c = pl.program_id(2)
    tm = hb_ref.shape[0]

    @pl.when(c == 0)
    def _():
        acc_ref[...] = jnp.zeros_like(acc_ref)

    act_ref[...] = lax.dot_general(u_ref[...], hb_ref[...], (((1,), (1,)), ((), ())),
                                   preferred_element_type=F32)
    for lt in range(tm // LANES):
        ls = slice(lt * LANES, (lt + 1) * LANES)
        for j in range(SUBLANES):
            w = jnp.zeros((N_KEYS, LANES), F32)
            for hd in range(PEER_HEADS):
                e2 = e2_ref[hd, :, ls]
                w = w + jnp.where(e2 >= tau_ref[hd, j:j + 1, ls], e2, 0.0) * e1_ref[hd, j:j + 1, ls]
            rs = slice(j * N_KEYS, (j + 1) * N_KEYS)
            p_ref[rs, ls] = (w * _gelu_tanh(act_ref[rs, ls])).astype(BF16)
    acc_ref[...] += jnp.dot(vt_ref[...], p_ref[...], preferred_element_type=F32)

    @pl.when(c == nc - 1)
    def _():
        y = ALPHA * x_ref[...] + gate_ref[...] * acc_ref[...].T
        o_ref[...] = _ln_rows(y, g_ref[...], b_ref[...])


def peer_experts(hb, e1, tau, e2, u, vt, x, gate, ln_g, ln_b, *, tm=512):
    g, l, d = x.shape
    tm = _pick(l, tm)
    nl = l // tm
    ec = EXPERT_CHUNK
    nc = N_EXPERTS // ec
    rspec = pl.BlockSpec((PEER_HEADS, SUBLANES, tm), lambda b, i, c: (0, c, b * nl + i))
    once = pl.Buffered(1)
    fspec = pl.BlockSpec((PEER_HEADS, N_KEYS, tm), lambda b, i, c: (0, 0, b * nl + i), pipeline_mode=once)
    tok_in = pl.BlockSpec((None, tm, d), lambda b, i, c: (b, i, 0), pipeline_mode=once)
    tok = pl.BlockSpec((None, tm, d), lambda b, i, c: (b, i, 0))
    vec = pl.BlockSpec((1, d), lambda b, i, c: (0, 0))
    return pl.pallas_call(
        functools.partial(_experts_kernel, nc=nc),
        out_shape=jax.ShapeDtypeStruct((g, l, d), F32),
        grid=(g, nl, nc),
        in_specs=[
            tok_in, rspec, rspec, fspec,
            pl.BlockSpec((ec, d), lambda b, i, c: (c, 0)),
            pl.BlockSpec((d, ec), lambda b, i, c: (0, c)),
            tok_in, pl.BlockSpec((None, 1, d), lambda b, i, c: (b, 0, 0)), vec, vec,
        ],
        out_specs=tok,
        scratch_shapes=[
            pltpu.VMEM((d, tm), F32),
            pltpu.VMEM((ec, tm), F32),
            pltpu.VMEM((ec, tm), BF16),
        ],
        compiler_params=_cparams(("parallel", "parallel", "arbitrary")),
    )(hb, e1, tau, e2, u, vt, x, gate, ln_g.reshape(1, d), ln_b.reshape(1, d))


def _rms_norm(x, g):
    return x * lax.rsqrt(jnp.mean(jnp.square(x), axis=-1, keepdims=True) + RMS_EPS) * g


def _rope_tables(l):
    rows = l // GRID_W
    row = jnp.repeat(jnp.arange(rows), GRID_W)
    col = jnp.tile(jnp.arange(GRID_W), rows)
    pos = jnp.stack([row, col], axis=-1).astype(F32)
    inv = ROPE_THETA ** (-jnp.arange(ROPE_PAIRS, dtype=F32) / ROPE_PAIRS)
    ang = pos[:, :, None] * inv
    return jnp.cos(ang), jnp.sin(ang)


def _rope_lane_tables(l):
    cos, sin = _rope_tables(l)
    c = jnp.broadcast_to(cos[:, :, None, :], (l, 2, 2, ROPE_PAIRS)).reshape(l, QK_ROPE)
    s = jnp.stack([-sin, sin], axis=2).reshape(l, QK_ROPE)
    c = jnp.tile(c, (1, MLA_HEADS))
    s = jnp.tile(s, (1, MLA_HEADS))
    return jnp.stack([jnp.ones_like(c), c]), jnp.stack([jnp.zeros_like(s), s])


def _extend_w_uq(w_uq):
    depth = w_uq.shape[0]
    w4 = w_uq.reshape(depth, Q_RANK, MLA_HEADS, QK_DIM)
    nope = w4[..., :QK_NOPE].reshape(depth, Q_RANK, NOPE_W)
    rope = w4[..., QK_NOPE:]
    swapped = jnp.flip(rope.reshape(depth, Q_RANK, MLA_HEADS, 2, 2, ROPE_PAIRS), axis=-2)
    return jnp.concatenate([nope, rope.reshape(depth, Q_RANK, ROPE_W),
                            swapped.reshape(depth, Q_RANK, ROPE_W)], axis=-1).astype(BF16)


def _apply_rope(x, cos, sin):
    b, l, h, _ = x.shape
    xr = x.reshape(b, l, h, 2, 2, ROPE_PAIRS)
    x1, x2 = xr[..., 0, :], xr[..., 1, :]
    c = cos[None, :, None]
    s = sin[None, :, None]
    out = jnp.stack([x1 * c - x2 * s, x2 * c + x1 * s], axis=-2)
    return out.reshape(b, l, h, QK_ROPE)


def _dft_matrices(l):
    n = 2 * l
    k = jnp.arange(l, dtype=jnp.int32)[:, None]
    t = jnp.arange(n, dtype=jnp.int32)[None, :]
    step = 64
    t_hi = jnp.arange(n // step, dtype=jnp.int32)[None, :] * step
    t_lo = jnp.arange(step, dtype=jnp.int32)[None, :]
    a_hi = ((k * t_hi) % n).astype(F32) * (2.0 * math.pi / n)
    a_lo = ((k * t_lo) % n).astype(F32) * (2.0 * math.pi / n)
    ch, sh = jnp.cos(a_hi)[:, :, None], jnp.sin(a_hi)[:, :, None]
    cl, sl = jnp.cos(a_lo)[:, None, :], jnp.sin(a_lo)[:, None, :]
    cos = (ch * cl - sh * sl).reshape(l, n)
    msin = -(sh * cl + ch * sl).reshape(l, n)
    nyq = jnp.where(t % 2 == 0, 1.0, -1.0).astype(F32)
    msin = jnp.where(k == 0, nyq, msin)
    fwd = jnp.concatenate([cos, msin], axis=0)
    wgt = jnp.where(k == 0, 1.0 / n, 2.0 / n)
    inv = jnp.concatenate([cos[:, :l] * wgt, msin[:, :l] * wgt], axis=0).T
    fwd = fwd.astype(BF16)
    return fwd, fwd[:, :l], inv.astype(BF16)


def _hyena_filter_spectrum(l, p, fwd):
    t = jnp.linspace(0.0, 1.0, l, dtype=F32)[:, None]
    bands = (HY_EMB - 1) // 2
    w = 2.0 * math.pi * jnp.arange(l, dtype=F32) / l
    fr_b = jnp.linspace(1e-4, bands - 1, bands, dtype=F32)
    ang = w[:, None] * fr_b[None]
    z = jnp.concatenate([t, jnp.cos(ang), -jnp.sin(ang)], axis=-1)
    z = jnp.concatenate([z, z[::-1]], axis=0)
    t2 = jnp.concatenate([t, t[::-1]], axis=0)
    zp = jnp.pad(z, ((0, 0), (0, LANES - HY_EMB)))
    w1 = jnp.pad(p['hy_f_w1'], ((0, LANES - HY_EMB), (0, LANES - HY_HIDDEN)))
    w2 = jnp.pad(p['hy_f_w2'], ((0, LANES - HY_HIDDEN), (0, LANES - HY_HIDDEN)))
    w3 = jnp.pad(p['hy_f_w3'], ((0, LANES - HY_HIDDEN), (0, 0)))
    pad_h = (0, LANES - HY_HIDDEN)
    fr = jnp.pad(p['hy_f_freq'], pad_h)
    hid = jnp.sin(fr * (matmul_f32(zp, w1) + jnp.pad(p['hy_f_b1'], pad_h)))
    hid = jnp.sin(fr * (matmul_f32(hid, w2) + jnp.pad(p['hy_f_b2'], pad_h)))
    half = HY_ORDER * HY_W
    decay = jnp.abs(p['hy_decay']).reshape(2, half)
    h_fwd = matmul_f32(hid[:l], w3[:, :half]) * jnp.exp(-t2[:l] * decay[0])
    h_bwd = matmul_f32(hid[l:], w3[:, half:]) * jnp.exp(-t2[l:] * decay[1])
    k = jnp.concatenate([h_fwd, jnp.zeros((1, half), F32), h_bwd[:l - 1]], axis=0)
    k = k.reshape(2 * l, HY_ORDER, HY_W)
    k = k / jnp.sum(jnp.abs(k), axis=0, keepdims=True)
    spec = matmul(fwd, k.reshape(2 * l, HY_ORDER * HY_W))
    kr, ki = spec[:l], spec[l:]
    first = (jnp.arange(l) == 0)[:, None]
    a = kr
    b = jnp.where(first, 0.0, ki)
    d = jnp.where(first, ki, kr)
    return a, b, d


def _long_conv(z, coef, fwd_half, inv, o):
    l = z.shape[1]
    a, b, d = (c[:, o * HY_W:(o + 1) * HY_W] for c in coef)
    spec = matmul(fwd_half, z.astype(BF16), tm=1024)
    xr, xi = spec[:, :l], spec[:, l:]
    y = jnp.concatenate([xr * a - xi * b, xr * b + xi * d], axis=1).astype(BF16)
    return matmul(inv, y, tm=1024)


def _hyena(u, p, dft, coef):
    _, fwd_half, inv = dft
    v, x1, x2 = jnp.split(u, 3, axis=-1)
    z = v
    for o, gate in enumerate((x1, x2)):
        z = gate * (_long_conv(z, coef, fwd_half, inv, o) + p['hy_bias'][o] * z)
    return z


def _mixers(proj, qf, nb, batch0, p, dft, coef, ctx_ckv, ctx_krope, rope):
    kv_lat = proj[batch0:batch0 + nb, :, Q_RANK:KR_OFF]
    k_r = proj[batch0:batch0 + nb, :, KR_OFF:KR_OFF + QK_ROPE]
    ckv = _rms_norm(kv_lat, p['kv_norm'])
    if rope is None:
        ckv_all, kr_all = ckv, k_r
    else:
        cos, sin = rope
        kr_rot = _apply_rope(k_r[:, :, None], cos, sin)[:, :, 0]
        ckv_all = jnp.concatenate([ctx_ckv, ckv], axis=1)
        kr_all = jnp.concatenate([ctx_krope, kr_rot], axis=1)
    kf, vf = kv_projection(ckv_all, kr_all, p['w_ukv'])
    att = attention(qf, kf, vf, q_batch0=batch0)
    sc = short_gated_conv(proj, p['sc_conv'], nb, batch0)
    hy = _hyena(hyena_input_conv(proj, p['hy_sconv_w'], p['hy_sconv_b'], nb, batch0), p, dft, coef)
    return jnp.concatenate([att, sc, hy], axis=-1), ckv, k_r


def _layer(x_all, p, cache_ckv_l, cache_krope_l, dfts, rope, n_ctx, ctx_len):
    g, lg, d = x_all.shape
    mod = p['mod']
    proj = in_projection(x_all, mod[:, 1:2], mod[:, 0:1], p['w_in'])
    coef_c = _hyena_filter_spectrum(ctx_len, p, dfts[0][0])
    coef_s = _hyena_filter_spectrum(lg, p, dfts[1][0])
    qf = q_projection(proj, p['q_norm'], p['w_uq'], rope[2], rope[3])
    q_ctx = qf[0].reshape(MLA_HEADS, n_ctx, ctx_len, QK_DIM).transpose(1, 0, 2, 3)
    cat_c, ckv, k_r = _mixers(proj[0].reshape(n_ctx, ctx_len, -1), q_ctx, n_ctx, 0, p, dfts[0], coef_c,
                              None, None, None)
    cat_s, _, _ = _mixers(proj, qf, g - 1, 1, p, dfts[1], coef_s, cache_ckv_l, cache_krope_l, rope[:2])
    cat = jnp.concatenate([cat_c.reshape(1, lg, d), cat_s], axis=0)
    x1 = out_projection(cat, x_all, mod[:, 2:3], p['w_out'], p['ln1_g'], p['ln1_b'])
    hb, e1, tau, e2 = peer_route(x1, mod[:, 4:5], mod[:, 3:4], p['wqt'], p['pkeys'])
    x2 = peer_experts(hb, e1, tau, e2, p['peer_u'], p['peer_vt'], x1, mod[:, 5:6], p['ln2_g'], p['ln2_b'])
    return x2, ckv, k_r


def kernel(x_prompt, x_sample, cache_ckv, cache_krope, c, c_ctx, w_mod, b_mod, w_in, mla_q_norm, mla_kv_norm, w_uq, w_ukv, sc_conv, hy_sconv_w, hy_sconv_b, hy_f_w1, hy_f_b1, hy_f_w2, hy_f_b2, hy_f_w3, hy_f_freq, hy_decay, hy_bias, w_out, ln1_g, ln1_b, ln2_g, ln2_b, pk_query, pk_keys, peer_u, peer_v):
    n_ctx, ctx_len, d = x_prompt.shape
    n_lat, lat_len, _ = x_sample.shape
    depth = w_in.shape[0]
    assert n_ctx * ctx_len == lat_len
    g = n_lat + 1

    cond = jnp.concatenate([c_ctx[None], c], axis=0)
    cond = jnp.pad(jax.nn.silu(cond), ((0, 16 - g), (0, 0)))
    mod = matmul(cond, w_mod, tm=16, tn=1536)[:, :g] + b_mod[:, None]
    mod = mod.reshape(depth, g, N_MOD, d)

    w_in_p = jnp.concatenate(
        [w_in[..., :KR_OFF + QK_ROPE],
         jnp.zeros((depth, d, SC_OFF - KR_OFF - QK_ROPE), w_in.dtype),
         w_in[..., KR_OFF + QK_ROPE:]], axis=-1).astype(BF16)
    layers = {
        'mod': mod, 'w_in': w_in_p, 'q_norm': mla_q_norm, 'kv_norm': mla_kv_norm,
        'w_uq': _extend_w_uq(w_uq), 'w_ukv': w_ukv.astype(BF16), 'sc_conv': sc_conv,
        'hy_sconv_w': hy_sconv_w, 'hy_sconv_b': hy_sconv_b, 'hy_f_w1': hy_f_w1, 'hy_f_b1': hy_f_b1,
        'hy_f_w2': hy_f_w2, 'hy_f_b2': hy_f_b2, 'hy_f_w3': hy_f_w3, 'hy_f_freq': hy_f_freq,
        'hy_decay': hy_decay, 'hy_bias': hy_bias, 'w_out': w_out.astype(BF16),
        'ln1_g': ln1_g, 'ln1_b': ln1_b, 'ln2_g': ln2_g, 'ln2_b': ln2_b,
        'wqt': jnp.swapaxes(pk_query, 1, 2).astype(BF16),
        'pkeys': pk_keys.reshape(depth, 2 * PEER_HEADS, N_KEYS, PK_HALF).astype(BF16),
        'peer_u': peer_u.astype(BF16),
        'peer_vt': jnp.swapaxes(peer_v.astype(BF16), 1, 2),
        'cache_ckv': jnp.swapaxes(cache_ckv, 0, 1),
        'cache_krope': jnp.swapaxes(cache_krope, 0, 1),
    }
    dfts = (_dft_matrices(ctx_len), _dft_matrices(lat_len))
    rope = _rope_tables(lat_len) + _rope_lane_tables(lat_len)

    def step(x_all, p):
        x_new, ckv, k_r = _layer(x_all, p, p['cache_ckv'], p['cache_krope'], dfts, rope, n_ctx, ctx_len)
        return x_new, (ckv, k_r)

    x_all = jnp.concatenate([x_prompt.reshape(1, lat_len, d), x_sample], axis=0)
    x_all, (ckv, k_r) = lax.scan(step, x_all, layers)
    y_prompt = x_all[0].reshape(n_ctx, ctx_len, d)
    y_sample = x_all[1:]
    return y_prompt, y_sample, jnp.swapaxes(ckv, 0, 1), jnp.swapaxes(k_r, 0, 1)
```

```python
import functools
import math

import jax
import jax.numpy as jnp
from jax import lax
from jax.experimental import pallas as pl
from jax.experimental.pallas import tpu as pltpu

F32 = jnp.float32
BF16 = jnp.bfloat16

D_MODEL = 2048
DEPTH = 4
GRID_W = 64
MLA_HEADS = 8
QK_NOPE = 128
QK_ROPE = 64
QK_DIM = QK_NOPE + QK_ROPE
V_DIM = 128
Q_RANK = D_MODEL // 4
KV_RANK = D_MODEL // 8
ROPE_PAIRS = QK_ROPE // 4
ROPE_THETA = 10000.0
SC_W = D_MODEL // 4
HY_W = D_MODEL // 4
HY_ORDER = 2
HY_EMB = 33
HY_HIDDEN = 64
PEER_HEADS = 8
N_KEYS = 128
N_EXPERTS = N_KEYS * N_KEYS
PK_HALF = 128
PEER_TOPK = 16
N_MOD = 6
ALPHA = (2 * DEPTH) ** 0.25
SM_SCALE = QK_DIM ** -0.5
LN_EPS = 1e-5
RMS_EPS = 1e-6

KR_OFF = Q_RANK + KV_RANK
SC_OFF = 1024
HY_OFF = SC_OFF + 3 * SC_W
IN_COLS_PAD = HY_OFF + 3 * HY_W

VMEM_LIMIT = 52 * 1024 * 1024
LANES = 128
SUBLANES = 8


def _cparams(sem):
    return pltpu.CompilerParams(dimension_semantics=sem, vmem_limit_bytes=VMEM_LIMIT)


def _mm_kernel(a_ref, b_ref, o_ref, acc_ref, *, nk):
    part = jnp.dot(a_ref[...].astype(BF16), b_ref[...].astype(BF16), preferred_element_type=F32)
    if nk == 1:
        o_ref[...] = part.astype(o_ref.dtype)
        return
    k = pl.program_id(3)

    @pl.when(k == 0)
    def _():
        acc_ref[...] = part

    @pl.when(k > 0)
    def _():
        acc_ref[...] += part

    @pl.when(k == nk - 1)
    def _():
        o_ref[...] = acc_ref[...].astype(o_ref.dtype)


def _pick(n, pref):
    t = min(n, pref)
    while n % t:
        t //= 2
    return t


def matmul(a, b, *, tm=512, tn=512, tk=2048, out_dtype=F32):
    squeeze = a.ndim == 2 and b.ndim == 2
    if a.ndim == 2:
        a = a[None]
    if b.ndim == 2:
        b = b[None]
    ba, m, k = a.shape
    bb, k2, n = b.shape
    assert k == k2 and (ba == bb or 1 in (ba, bb))
    nb = max(ba, bb)
    tm, tn, tk = _pick(m, tm), _pick(n, tn), _pick(k, tk)
    nk = k // tk
    out = pl.pallas_call(
        functools.partial(_mm_kernel, nk=nk),
        out_shape=jax.ShapeDtypeStruct((nb, m, n), out_dtype),
        grid=(nb, m // tm, n // tn, nk),
        in_specs=[
            pl.BlockSpec((None, tm, tk), lambda g, i, j, kk: (g if ba > 1 else 0, i, kk)),
            pl.BlockSpec((None, tk, tn), lambda g, i, j, kk: (g if bb > 1 else 0, kk, j)),
        ],
        out_specs=pl.BlockSpec((None, tm, tn), lambda g, i, j, kk: (g, i, j)),
        scratch_shapes=[pltpu.VMEM((tm, tn), F32)],
        compiler_params=_cparams(("parallel", "parallel", "parallel", "arbitrary")),
    )(a, b)
    return out[0] if squeeze else out


def _mm_f32_kernel(a_ref, b_ref, o_ref):
    o_ref[...] = jnp.dot(a_ref[...], b_ref[...], preferred_element_type=F32,
                         precision=lax.Precision.HIGHEST)


def matmul_f32(a, b, *, tm=512):
    m, k = a.shape
    n = b.shape[1]
    tm = _pick(m, tm)
    return pl.pallas_call(
        _mm_f32_kernel,
        out_shape=jax.ShapeDtypeStruct((m, n), F32),
        grid=(m // tm,),
        in_specs=[pl.BlockSpec((tm, k), lambda i: (i, 0)), pl.BlockSpec((k, n), lambda i: (0, 0))],
        out_specs=pl.BlockSpec((tm, n), lambda i: (i, 0)),
        compiler_params=_cparams(("parallel",)),
    )(a, b)


def _inproj_kernel(x_ref, sc_ref, sh_ref, w_ref, o_ref):
    h = x_ref[...] * (1.0 + sc_ref[...]) + sh_ref[...]
    o_ref[...] = jnp.dot(h.astype(BF16), w_ref[...], preferred_element_type=F32)


def in_projection(x, scale, shift, w, *, tm=512, tn=2048):
    g, l, d = x.shape
    n = w.shape[1]
    tm, tn = _pick(l, tm), _pick(n, tn)
    return pl.pallas_call(
        _inproj_kernel,
        out_shape=jax.ShapeDtypeStruct((g, l, n), F32),
        grid=(g, l // tm, n // tn),
        in_specs=[
            pl.BlockSpec((None, tm, d), lambda b, i, j: (b, i, 0)),
            pl.BlockSpec((None, 1, d), lambda b, i, j: (b, 0, 0)),
            pl.BlockSpec((None, 1, d), lambda b, i, j: (b, 0, 0)),
            pl.BlockSpec((d, tn), lambda b, i, j: (0, j)),
        ],
        out_specs=pl.BlockSpec((None, tm, tn), lambda b, i, j: (b, i, j)),
        compiler_params=_cparams(("parallel", "parallel", "arbitrary")),
    )(x, scale, shift, w)


def _ln_rows(y, g, b):
    mu = jnp.mean(y, axis=-1, keepdims=True)
    yc = y - mu
    var = jnp.mean(yc * yc, axis=-1, keepdims=True)
    return yc * lax.rsqrt(var + LN_EPS) * g + b


def _outproj_kernel(cat_ref, x_ref, gate_ref, w_ref, g_ref, b_ref, o_ref):
    mix = jnp.dot(cat_ref[...].astype(BF16), w_ref[...], preferred_element_type=F32)
    o_ref[...] = _ln_rows(ALPHA * x_ref[...] + gate_ref[...] * mix, g_ref[...], b_ref[...])


def out_projection(cat, x, gate, w, ln_g, ln_b, *, tm=512):
    g, l, d = x.shape
    tm = _pick(l, tm)
    tok = pl.BlockSpec((None, tm, d), lambda b, i: (b, i, 0))
    vec = pl.BlockSpec((1, d), lambda b, i: (0, 0))
    return pl.pallas_call(
        _outproj_kernel,
        out_shape=jax.ShapeDtypeStruct((g, l, d), F32),
        grid=(g, l // tm),
        in_specs=[tok, tok, pl.BlockSpec((None, 1, d), lambda b, i: (b, 0, 0)),
                  pl.BlockSpec((d, d), lambda b, i: (0, 0)), vec, vec],
        out_specs=tok,
        compiler_params=_cparams(("parallel", "parallel")),
    )(cat, x, gate, w, ln_g.reshape(1, d), ln_b.reshape(1, d))


def _conv3_rows(u, w):
    l = u.shape[0]
    row = lax.broadcasted_iota(jnp.int32, u.shape, 0)
    prev = jnp.where(row == 0, 0.0, pltpu.roll(u, 1, 0))
    nxt = jnp.where(row == l - 1, 0.0, pltpu.roll(u, l - 1, 0))
    return prev * w[0:1] + u * w[1:2] + nxt * w[2:3]


def _gated_conv_kernel(bg_ref, cg_ref, hs_ref, w_ref, o_ref):
    o_ref[...] = bg_ref[...] * _conv3_rows(cg_ref[...] * hs_ref[...], w_ref[...])


def _bias_conv_kernel(x_ref, w_ref, b_ref, o_ref):
    o_ref[...] = _conv3_rows(x_ref[...], w_ref[...]) + b_ref[...]


def _col_spec(l, batch0, col0):
    return pl.BlockSpec((None, l, LANES), lambda b, j: (b + batch0, 0, col0 + j))


def short_gated_conv(proj, w, nb, batch0):
    l = proj.shape[1]
    c0 = SC_OFF // LANES
    nj = SC_W // LANES
    return pl.pallas_call(
        _gated_conv_kernel,
        out_shape=jax.ShapeDtypeStruct((nb, l, SC_W), F32),
        grid=(nb, nj),
        in_specs=[_col_spec(l, batch0, c0), _col_spec(l, batch0, c0 + nj), _col_spec(l, batch0, c0 + 2 * nj),
                  pl.BlockSpec((3, LANES), lambda b, j: (0, j))],
        out_specs=pl.BlockSpec((None, l, LANES), lambda b, j: (b, 0, j)),
        compiler_params=_cparams(("parallel", "parallel")),
    )(proj, proj, proj, w)


def hyena_input_conv(proj, w, bias, nb, batch0):
    l = proj.shape[1]
    width = 3 * HY_W
    return pl.pallas_call(
        _bias_conv_kernel,
        out_shape=jax.ShapeDtypeStruct((nb, l, width), F32),
        grid=(nb, width // LANES),
        in_specs=[_col_spec(l, batch0, HY_OFF // LANES),
                  pl.BlockSpec((3, LANES), lambda b, j: (0, j)),
                  pl.BlockSpec((1, LANES), lambda b, j: (0, j))],
        out_specs=pl.BlockSpec((None, l, LANES), lambda b, j: (b, 0, j)),
        compiler_params=_cparams(("parallel", "parallel")),
    )(proj, w, bias.reshape(1, width))


NOPE_W = MLA_HEADS * QK_NOPE
ROPE_W = MLA_HEADS * QK_ROPE


def _qproj_kernel(p_ref, g_ref, w_ref, c_ref, s_ref, o_ref):
    x = p_ref[...]
    xn = x * lax.rsqrt(jnp.mean(x * x, axis=-1, keepdims=True) + RMS_EPS) * g_ref[...]
    acc = jnp.dot(xn.astype(BF16), w_ref[...], preferred_element_type=F32)
    rope = acc[:, NOPE_W:NOPE_W + ROPE_W] * c_ref[...] + acc[:, NOPE_W + ROPE_W:] * s_ref[...]
    for h in range(MLA_HEADS):
        o_ref[h, :, 0:QK_NOPE] = (acc[:, h * QK_NOPE:(h + 1) * QK_NOPE] * SM_SCALE).astype(BF16)
        o_ref[h, :, QK_NOPE:QK_DIM] = (rope[:, h * QK_ROPE:(h + 1) * QK_ROPE] * SM_SCALE).astype(BF16)


def q_projection(proj, g_norm, w, rope_c, rope_s, *, tm=512):
    g, l, _ = proj.shape
    tm = _pick(l, tm)
    tab = pl.BlockSpec((None, tm, ROPE_W), lambda b, i: (jnp.minimum(b, 1), i, 0))
    return pl.pallas_call(
        _qproj_kernel,
        out_shape=jax.ShapeDtypeStruct((g, MLA_HEADS, l, QK_DIM), BF16),
        grid=(g, l // tm),
        in_specs=[
            pl.BlockSpec((None, tm, Q_RANK), lambda b, i: (b, i, 0)),
            pl.BlockSpec((1, Q_RANK), lambda b, i: (0, 0)),
            pl.BlockSpec(w.shape, lambda b, i: (0, 0)),
            tab, tab,
        ],
        out_specs=pl.BlockSpec((None, MLA_HEADS, tm, QK_DIM), lambda b, i: (b, 0, i, 0)),
        compiler_params=_cparams(("parallel", "parallel")),
    )(proj, g_norm.reshape(1, Q_RANK), w, rope_c, rope_s)


def _kvproj_kernel(c_ref, kr_ref, w_ref, k_ref, v_ref):
    kv = jnp.dot(c_ref[...].astype(BF16), w_ref[...], preferred_element_type=F32)
    kr = kr_ref[...].astype(BF16)
    width = QK_NOPE + V_DIM
    for h in range(MLA_HEADS):
        k_ref[h, :, 0:QK_NOPE] = kv[:, h * width:h * width + QK_NOPE].astype(BF16)
        k_ref[h, :, QK_NOPE:QK_DIM] = kr
        v_ref[h] = kv[:, h * width + QK_NOPE:(h + 1) * width].astype(BF16)


def kv_projection(ckv, kr, w, *, tm=256):
    b, lk, _ = ckv.shape
    tm = _pick(lk, tm)
    return pl.pallas_call(
        _kvproj_kernel,
        out_shape=(jax.ShapeDtypeStruct((b, MLA_HEADS, lk, QK_DIM), BF16),
                   jax.ShapeDtypeStruct((b, MLA_HEADS, lk, V_DIM), BF16)),
        grid=(b, lk // tm),
        in_specs=[
            pl.BlockSpec((None, tm, KV_RANK), lambda bi, i: (bi, i, 0)),
            pl.BlockSpec((None, tm, QK_ROPE), lambda bi, i: (bi, i, 0)),
            pl.BlockSpec(w.shape, lambda bi, i: (0, 0)),
        ],
        out_specs=(pl.BlockSpec((None, MLA_HEADS, tm, QK_DIM), lambda bi, i: (bi, 0, i, 0)),
                   pl.BlockSpec((None, MLA_HEADS, tm, V_DIM), lambda bi, i: (bi, 0, i, 0))),
        compiler_params=_cparams(("parallel", "parallel")),
    )(ckv, kr, w)


def _attn_kernel(q_ref, k_ref, v_ref, o_ref):
    s = lax.dot_general(q_ref[...], k_ref[...], (((1,), (1,)), ((), ())),
                        preferred_element_type=F32)
    m = jnp.max(s, axis=-1, keepdims=True)
    p = jnp.exp(s - m)
    l = jnp.sum(p, axis=-1, keepdims=True)
    o = jnp.dot(p.astype(BF16), v_ref[...], preferred_element_type=F32)
    o_ref[...] = o / l


def attention(q, k, v, *, q_batch0=0, tq=256):
    b, h, lk, dq = k.shape
    l = q.shape[2]
    tq = _pick(l, tq)
    return pl.pallas_call(
        _attn_kernel,
        out_shape=jax.ShapeDtypeStruct((b, l, h * V_DIM), F32),
        grid=(b, h, l // tq),
        in_specs=[
            pl.BlockSpec((None, None, tq, dq), lambda bi, hi, i: (bi + q_batch0, hi, i, 0)),
            pl.BlockSpec((None, None, lk, dq), lambda bi, hi, i: (bi, hi, 0, 0)),
            pl.BlockSpec((None, None, lk, V_DIM), lambda bi, hi, i: (bi, hi, 0, 0)),
        ],
        out_specs=pl.BlockSpec((None, tq, V_DIM), lambda bi, hi, i: (bi, i, hi)),
        compiler_params=_cparams(("parallel", "parallel", "arbitrary")),
    )(q, k, v)


NEG_INF = float("-inf")


def _exchange(xs, i, j):
    xs[i], xs[j] = jnp.maximum(xs[i], xs[j]), jnp.minimum(xs[i], xs[j])


def _bitonic_merge_desc(xs):
    n = len(xs)
    j = n // 2
    while j >= 1:
        for i in range(n):
            if i & j == 0:
                _exchange(xs, i, i | j)
        j //= 2


def _sort_desc(xs):
    n = len(xs)
    k = 2
    while k <= n:
        j = k // 2
        while j >= 1:
            for i in range(n):
                l = i ^ j
                if l > i:
                    if i & k == 0:
                        _exchange(xs, i, l)
                    else:
                        _exchange(xs, l, i)
            j //= 2
        k *= 2


def _all_sublanes_max(m):
    for shift in (4, 2, 1):
        m = jnp.maximum(m, pltpu.roll(m, shift, 0))
    return m


def _top16_sorted(s):
    xs = [s[v * SUBLANES:(v + 1) * SUBLANES] for v in range(N_KEYS // SUBLANES)]
    _sort_desc(xs)
    for shift in (4, 2, 1):
        xs = [jnp.maximum(xs[r], pltpu.roll(xs[PEER_TOPK - 1 - r], shift, 0)) for r in range(PEER_TOPK)]
        _bitonic_merge_desc(xs)
    return xs


def _spread(rows, sub):
    out = rows[0]
    for b in range(1, SUBLANES):
        out = jnp.where(sub == b, rows[b], out)
    return out


def _all_sublanes_min(m):
    for shift in (4, 2, 1):
        m = jnp.minimum(m, pltpu.roll(m, shift, 0))
    return m


def _best_sums(cands, sub, count):
    ids = [sub + g * SUBLANES for g in range(len(cands))]
    far = len(cands) * SUBLANES
    best = []
    for _ in range(count):
        m = cands[0]
        for cnd in cands[1:]:
            m = jnp.maximum(m, cnd)
        m = _all_sublanes_max(m)
        best.append(m)
        first = jnp.where(cands[0] == m, ids[0], far)
        for cnd, idg in zip(cands[1:], ids[1:]):
            first = jnp.minimum(first, jnp.where(cnd == m, idg, far))
        first = _all_sublanes_min(first)
        cands = [jnp.where(idg == first, NEG_INF, cnd) for cnd, idg in zip(cands, ids)]
    return best


def _route_kernel(x_ref, sc_ref, sh_ref, wqt_ref, pk_ref, hb_ref, e1_ref, tau_ref, e2_ref, qt_ref, s_ref):
    tm = x_ref.shape[0]
    h = x_ref[...] * (1.0 + sc_ref[...]) + sh_ref[...]
    hb = h.astype(BF16)
    hb_ref[...] = hb
    qt = lax.dot_general(wqt_ref[...], hb, (((1,), (1,)), ((), ())), preferred_element_type=F32)
    for hp in range(2 * PEER_HEADS):
        qt_ref[hp] = qt[hp * PK_HALF:(hp + 1) * PK_HALF].astype(BF16)

    def head(hd, carry):
        for p in range(2):
            s_ref[p] = jnp.dot(pk_ref[2 * hd + p], qt_ref[2 * hd + p], preferred_element_type=F32)
        for lt in range(tm // LANES):
            ls = slice(lt * LANES, (lt + 1) * LANES)
            s1 = s_ref[0, :, ls]
            s2 = s_ref[1, :, ls]
            v1 = _top16_sorted(s1)
            v2 = _top16_sorted(s2)
            sub = lax.broadcasted_iota(jnp.int32, (SUBLANES, LANES), 0)
            v2_lo, v2_hi = _spread(v2[:SUBLANES], sub), _spread(v2[SUBLANES:], sub)
            v1_hi = _spread(v1[SUBLANES:], sub)
            cands = [v1[0] + v2_lo, v1[0] + v2_hi, v1_hi + v2[0]]
            cands += [v1[a] + v2_lo for a in range(1, SUBLANES)]
            best = _best_sums(cands, sub, PEER_TOPK + 1)
            z = jnp.ones_like(best[0])
            for k in range(1, PEER_TOPK):
                z = z + jnp.exp(best[k] - best[0])
            thr = 0.5 * (best[PEER_TOPK - 1] + best[PEER_TOPK])
            m1 = v1[0][0:1]
            m2 = v2[0][0:1]
            in1 = s1 >= v1[PEER_TOPK - 1][0:1]
            in2 = s2 >= v2[PEER_TOPK - 1][0:1]
            e1_ref[hd, :, ls] = jnp.exp(s1 - m1) / z[0:1]
            e2_ref[hd, :, ls] = jnp.where(in2, jnp.exp(s2 - m2), 0.0)
            tau_ref[hd, :, ls] = jnp.where(in1, jnp.exp((thr[0:1] - m2) - s1), jnp.inf)
        return carry

    lax.fori_loop(0, PEER_HEADS, head, 0)


def peer_route(x, scale, shift, wqt, pkeys, *, tm=256):
    g, l, d = x.shape
    t = g * l
    tm = _pick(l, tm)
    nl = l // tm
    fac = jax.ShapeDtypeStruct((PEER_HEADS, N_KEYS, t), F32)
    fspec = pl.BlockSpec((PEER_HEADS, N_KEYS, tm), lambda b, i: (0, 0, b * nl + i))
    return pl.pallas_call(
        _route_kernel,
        out_shape=(jax.ShapeDtypeStruct((g, l, d), BF16), fac, fac, fac),
        grid=(g, nl),
        in_specs=[
            pl.BlockSpec((None, tm, d), lambda b, i: (b, i, 0)),
            pl.BlockSpec((None, 1, d), lambda b, i: (b, 0, 0)),
            pl.BlockSpec((None, 1, d), lambda b, i: (b, 0, 0)),
            pl.BlockSpec(wqt.shape, lambda b, i: (0, 0)),
            pl.BlockSpec(pkeys.shape, lambda b, i: (0, 0, 0)),
        ],
        out_specs=(pl.BlockSpec((None, tm, d), lambda b, i: (b, i, 0)), fspec, fspec, fspec),
        scratch_shapes=[
            pltpu.VMEM((2 * PEER_HEADS, PK_HALF, tm), BF16),
            pltpu.VMEM((2, N_KEYS, tm), F32),
        ],
        compiler_params=_cparams(("parallel", "arbitrary")),
    )(x, scale, shift, wqt, pkeys)


GELU_C = math.sqrt(2.0 / math.pi)


def _gelu_tanh(a):
    half = 0.5 * a
    return half + half * jnp.tanh(a * (GELU_C + (GELU_C * 0.044715) * (a * a)))


EXPERT_CHUNK = SUBLANES * N_KEYS


def _experts_kernel(hb_ref, e1_ref, tau_ref, e2_ref, u_ref, vt_ref, x_ref, gate_ref, g_ref, b_ref,
                    o_ref, acc_ref, act_ref, p_ref, *, nc):
    c = pl.program_id(2)
    tm = hb_ref.shape[0]

    @pl.when(c == 0)
    def _():
        acc_ref[...] = jnp.zeros_like(acc_ref)

    act_ref[...] = lax.dot_general(u_ref[...], hb_ref[...], (((1,), (1,)), ((), ())),
                                   preferred_element_type=F32)
    for lt in range(tm // LANES):
        ls = slice(lt * LANES, (lt + 1) * LANES)
        for j in range(SUBLANES):
            w = jnp.zeros((N_KEYS, LANES), F32)
            for hd in range(PEER_HEADS):
                e2 = e2_ref[hd, :, ls]
                w = w + jnp.where(e2 >= tau_ref[hd, j:j + 1, ls], e2, 0.0) * e1_ref[hd, j:j + 1, ls]
            rs = slice(j * N_KEYS, (j + 1) * N_KEYS)
            p_ref[rs, ls] = (w * _gelu_tanh(act_ref[rs, ls])).astype(BF16)
    acc_ref[...] += jnp.dot(vt_ref[...], p_ref[...], preferred_element_type=F32)

    @pl.when(c == nc - 1)
    def _():
        y = ALPHA * x_ref[...] + gate_ref[...] * acc_ref[...].T
        o_ref[...] = _ln_rows(y, g_ref[...], b_ref[...])


def peer_experts(hb, e1, tau, e2, u, vt, x, gate, ln_g, ln_b, *, tm=512):
    g, l, d = x.shape
    tm = _pick(l, tm)
    nl = l // tm
    ec = EXPERT_CHUNK
    nc = N_EXPERTS // ec
    rspec = pl.BlockSpec((PEER_HEADS, SUBLANES, tm), lambda b, i, c: (0, c, b * nl + i))
    once = pl.Buffered(1)
    fspec = pl.BlockSpec((PEER_HEADS, N_KEYS, tm), lambda b, i, c: (0, 0, b * nl + i), pipeline_mode=once)
    tok_in = pl.BlockSpec((None, tm, d), lambda b, i, c: (b, i, 0), pipeline_mode=once)
    tok = pl.BlockSpec((None, tm, d), lambda b, i, c: (b, i, 0))
    vec = pl.BlockSpec((1, d), lambda b, i, c: (0, 0))
    return pl.pallas_call(
        functools.partial(_experts_kernel, nc=nc),
        out_shape=jax.ShapeDtypeStruct((g, l, d), F32),
        grid=(g, nl, nc),
        in_specs=[
            tok_in, rspec, rspec, fspec,
            pl.BlockSpec((ec, d), lambda b, i, c: (c, 0)),
            pl.BlockSpec((d, ec), lambda b, i, c: (0, c)),
            tok_in, pl.BlockSpec((None, 1, d), lambda b, i, c: (b, 0, 0)), vec, vec,
        ],
        out_specs=tok,
        scratch_shapes=[
            pltpu.VMEM((d, tm), F32),
            pltpu.VMEM((ec, tm), F32),
            pltpu.VMEM((ec, tm), BF16),
        ],
        compiler_params=_cparams(("parallel", "parallel", "arbitrary")),
    )(hb, e1, tau, e2, u, vt, x, gate, ln_g.reshape(1, d), ln_b.reshape(1, d))


def _rms_norm(x, g):
    return x * lax.rsqrt(jnp.mean(jnp.square(x), axis=-1, keepdims=True) + RMS_EPS) * g


def _rope_tables(l):
    rows = l // GRID_W
    row = jnp.repeat(jnp.arange(rows), GRID_W)
    col = jnp.tile(jnp.arange(GRID_W), rows)
    pos = jnp.stack([row, col], axis=-1).astype(F32)
    inv = ROPE_THETA ** (-jnp.arange(ROPE_PAIRS, dtype=F32) / ROPE_PAIRS)
    ang = pos[:, :, None] * inv
    return jnp.cos(ang), jnp.sin(ang)


def _rope_lane_tables(l):
    cos, sin = _rope_tables(l)
    c = jnp.broadcast_to(cos[:, :, None, :], (l, 2, 2, ROPE_PAIRS)).reshape(l, QK_ROPE)
    s = jnp.stack([-sin, sin], axis=2).reshape(l, QK_ROPE)
    c = jnp.tile(c, (1, MLA_HEADS))
    s = jnp.tile(s, (1, MLA_HEADS))
    return jnp.stack([jnp.ones_like(c), c]), jnp.stack([jnp.zeros_like(s), s])


def _extend_w_uq(w_uq):
    depth = w_uq.shape[0]
    w4 = w_uq.reshape(depth, Q_RANK, MLA_HEADS, QK_DIM)
    nope = w4[..., :QK_NOPE].reshape(depth, Q_RANK, NOPE_W)
    rope = w4[..., QK_NOPE:]
    swapped = jnp.flip(rope.reshape(depth, Q_RANK, MLA_HEADS, 2, 2, ROPE_PAIRS), axis=-2)
    return jnp.concatenate([nope, rope.reshape(depth, Q_RANK, ROPE_W),
                            swapped.reshape(depth, Q_RANK, ROPE_W)], axis=-1).astype(BF16)


def _apply_rope(x, cos, sin):
    b, l, h, _ = x.shape
    xr = x.reshape(b, l, h, 2, 2, ROPE_PAIRS)
    x1, x2 = xr[..., 0, :], xr[..., 1, :]
    c = cos[None, :, None]
    s = sin[None, :, None]
    out = jnp.stack([x1 * c - x2 * s, x2 * c + x1 * s], axis=-2)
    return out.reshape(b, l, h, QK_ROPE)


def _dft_matrices(l):
    n = 2 * l
    k = jnp.arange(l, dtype=jnp.int32)[:, None]
    t = jnp.arange(n, dtype=jnp.int32)[None, :]
    step = 64
    t_hi = jnp.arange(n // step, dtype=jnp.int32)[None, :] * step
    t_lo = jnp.arange(step, dtype=jnp.int32)[None, :]
    a_hi = ((k * t_hi) % n).astype(F32) * (2.0 * math.pi / n)
    a_lo = ((k * t_lo) % n).astype(F32) * (2.0 * math.pi / n)
    ch, sh = jnp.cos(a_hi)[:, :, None], jnp.sin(a_hi)[:, :, None]
    cl, sl = jnp.cos(a_lo)[:, None, :], jnp.sin(a_lo)[:, None, :]
    cos = (ch * cl - sh * sl).reshape(l, n)
    msin = -(sh * cl + ch * sl).reshape(l, n)
    nyq = jnp.where(t % 2 == 0, 1.0, -1.0).astype(F32)
    msin = jnp.where(k == 0, nyq, msin)
    fwd = jnp.concatenate([cos, msin], axis=0)
    wgt = jnp.where(k == 0, 1.0 / n, 2.0 / n)
    inv = jnp.concatenate([cos[:, :l] * wgt, msin[:, :l] * wgt], axis=0).T
    fwd = fwd.astype(BF16)
    return fwd, fwd[:, :l], inv.astype(BF16)


def _hyena_filter_spectrum(l, p, fwd):
    t = jnp.linspace(0.0, 1.0, l, dtype=F32)[:, None]
    bands = (HY_EMB - 1) // 2
    w = 2.0 * math.pi * jnp.arange(l, dtype=F32) / l
    fr_b = jnp.linspace(1e-4, bands - 1, bands, dtype=F32)
    ang = w[:, None] * fr_b[None]
    z = jnp.concatenate([t, jnp.cos(ang), -jnp.sin(ang)], axis=-1)
    z = jnp.concatenate([z, z[::-1]], axis=0)
    t2 = jnp.concatenate([t, t[::-1]], axis=0)
    zp = jnp.pad(z, ((0, 0), (0, LANES - HY_EMB)))
    w1 = jnp.pad(p['hy_f_w1'], ((0, LANES - HY_EMB), (0, LANES - HY_HIDDEN)))
    w2 = jnp.pad(p['hy_f_w2'], ((0, LANES - HY_HIDDEN), (0, LANES - HY_HIDDEN)))
    w3 = jnp.pad(p['hy_f_w3'], ((0, LANES - HY_HIDDEN), (0, 0)))
    pad_h = (0, LANES - HY_HIDDEN)
    fr = jnp.pad(p['hy_f_freq'], pad_h)
    hid = jnp.sin(fr * (matmul_f32(zp, w1) + jnp.pad(p['hy_f_b1'], pad_h)))
    hid = jnp.sin(fr * (matmul_f32(hid, w2) + jnp.pad(p['hy_f_b2'], pad_h)))
    half = HY_ORDER * HY_W
    decay = jnp.abs(p['hy_decay']).reshape(2, half)
    h_fwd = matmul_f32(hid[:l], w3[:, :half]) * jnp.exp(-t2[:l] * decay[0])
    h_bwd = matmul_f32(hid[l:], w3[:, half:]) * jnp.exp(-t2[l:] * decay[1])
    k = jnp.concatenate([h_fwd, jnp.zeros((1, half), F32), h_bwd[:l - 1]], axis=0)
    k = k.reshape(2 * l, HY_ORDER, HY_W)
    k = k / jnp.sum(jnp.abs(k), axis=0, keepdims=True)
    spec = matmul(fwd, k.reshape(2 * l, HY_ORDER * HY_W))
    kr, ki = spec[:l], spec[l:]
    first = (jnp.arange(l) == 0)[:, None]
    a = kr
    b = jnp.where(first, 0.0, ki)
    d = jnp.where(first, ki, kr)
    return a, b, d


def _long_conv(z, coef, fwd_half, inv, o):
    l = z.shape[1]
    a, b, d = (c[:, o * HY_W:(o + 1) * HY_W] for c in coef)
    spec = matmul(fwd_half, z.astype(BF16), tm=1024)
    xr, xi = spec[:, :l], spec[:, l:]
    y = jnp.concatenate([xr * a - xi * b, xr * b + xi * d], axis=1).astype(BF16)
    return matmul(inv, y, tm=1024)


def _hyena(u, p, dft, coef):
    _, fwd_half, inv = dft
    v, x1, x2 = jnp.split(u, 3, axis=-1)
    z = v
    for o, gate in enumerate((x1, x2)):
        z = gate * (_long_conv(z, coef, fwd_half, inv, o) + p['hy_bias'][o] * z)
    return z


def _mixers(proj, qf, nb, batch0, p, dft, coef, ctx_ckv, ctx_krope, rope):
    kv_lat = proj[batch0:batch0 + nb, :, Q_RANK:KR_OFF]
    k_r = proj[batch0:batch0 + nb, :, KR_OFF:KR_OFF + QK_ROPE]
    ckv = _rms_norm(kv_lat, p['kv_norm'])
    if rope is None:
        ckv_all, kr_all = ckv, k_r
    else:
        cos, sin = rope
        kr_rot = _apply_rope(k_r[:, :, None], cos, sin)[:, :, 0]
        ckv_all = jnp.concatenate([ctx_ckv, ckv], axis=1)
        kr_all = jnp.concatenate([ctx_krope, kr_rot], axis=1)
    kf, vf = kv_projection(ckv_all, kr_all, p['w_ukv'])
    att = attention(qf, kf, vf, q_batch0=batch0)
    sc = short_gated_conv(proj, p['sc_conv'], nb, batch0)
    hy = _hyena(hyena_input_conv(proj, p['hy_sconv_w'], p['hy_sconv_b'], nb, batch0), p, dft, coef)
    return jnp.concatenate([att, sc, hy], axis=-1), ckv, k_r


def _layer(x_all, p, cache_ckv_l, cache_krope_l, dfts, rope, n_ctx, ctx_len):
    g, lg, d = x_all.shape
    mod = p['mod']
    proj = in_projection(x_all, mod[:, 1:2], mod[:, 0:1], p['w_in'])
    coef_c = _hyena_filter_spectrum(ctx_len, p, dfts[0][0])
    coef_s = _hyena_filter_spectrum(lg, p, dfts[1][0])
    qf = q_projection(proj, p['q_norm'], p['w_uq'], rope[2], rope[3])
    q_ctx = qf[0].reshape(MLA_HEADS, n_ctx, ctx_len, QK_DIM).transpose(1, 0, 2, 3)
    cat_c, ckv, k_r = _mixers(proj[0].reshape(n_ctx, ctx_len, -1), q_ctx, n_ctx, 0, p, dfts[0], coef_c,
                              None, None, None)
    cat_s, _, _ = _mixers(proj, qf, g - 1, 1, p, dfts[1], coef_s, cache_ckv_l, cache_krope_l, rope[:2])
    cat = jnp.concatenate([cat_c.reshape(1, lg, d), cat_s], axis=0)
    x1 = out_projection(cat, x_all, mod[:, 2:3], p['w_out'], p['ln1_g'], p['ln1_b'])
    hb, e1, tau, e2 = peer_route(x1, mod[:, 4:5], mod[:, 3:4], p['wqt'], p['pkeys'])
    x2 = peer_experts(hb, e1, tau, e2, p['peer_u'], p['peer_vt'], x1, mod[:, 5:6], p['ln2_g'], p['ln2_b'])
    return x2, ckv, k_r


def kernel(x_prompt, x_sample, cache_ckv, cache_krope, c, c_ctx, w_mod, b_mod, w_in, mla_q_norm, mla_kv_norm, w_uq, w_ukv, sc_conv, hy_sconv_w, hy_sconv_b, hy_f_w1, hy_f_b1, hy_f_w2, hy_f_b2, hy_f_w3, hy_f_freq, hy_decay, hy_bias, w_out, ln1_g, ln1_b, ln2_g, ln2_b, pk_query, pk_keys, peer_u, peer_v):
    n_ctx, ctx_len, d = x_prompt.shape
    n_lat, lat_len, _ = x_sample.shape
    depth = w_in.shape[0]
    assert n_ctx * ctx_len == lat_len
    g = n_lat + 1

    cond = jnp.concatenate([c_ctx[None], c], axis=0)
    cond = jnp.pad(jax.nn.silu(cond), ((0, 16 - g), (0, 0)))
    mod = matmul(cond, w_mod, tm=16, tn=1536)[:, :g] + b_mod[:, None]
    mod = mod.reshape(depth, g, N_MOD, d)

    w_in_p = jnp.concatenate(
        [w_in[..., :KR_OFF + QK_ROPE],
         jnp.zeros((depth, d, SC_OFF - KR_OFF - QK_ROPE), w_in.dtype),
         w_in[..., KR_OFF + QK_ROPE:]], axis=-1).astype(BF16)
    layers = {
        'mod': mod, 'w_in': w_in_p, 'q_norm': mla_q_norm, 'kv_norm': mla_kv_norm,
        'w_uq': _extend_w_uq(w_uq), 'w_ukv': w_ukv.astype(BF16), 'sc_conv': sc_conv,
        'hy_sconv_w': hy_sconv_w, 'hy_sconv_b': hy_sconv_b, 'hy_f_w1': hy_f_w1, 'hy_f_b1': hy_f_b1,
        'hy_f_w2': hy_f_w2, 'hy_f_b2': hy_f_b2, 'hy_f_w3': hy_f_w3, 'hy_f_freq': hy_f_freq,
        'hy_decay': hy_decay, 'hy_bias': hy_bias, 'w_out': w_out.astype(BF16),
        'ln1_g': ln1_g, 'ln1_b': ln1_b, 'ln2_g': ln2_g, 'ln2_b': ln2_b,
        'wqt': jnp.swapaxes(pk_query, 1, 2).astype(BF16),
        'pkeys': pk_keys.reshape(depth, 2 * PEER_HEADS, N_KEYS, PK_HALF).astype(BF16),
        'peer_u': peer_u.astype(BF16),
        'peer_vt': jnp.swapaxes(peer_v.astype(BF16), 1, 2),
        'cache_ckv': jnp.swapaxes(cache_ckv, 0, 1),
        'cache_krope': jnp.swapaxes(cache_krope, 0, 1),
    }
    dfts = (_dft_matrices(ctx_len), _dft_matrices(lat_len))
    rope = _rope_tables(lat_len) + _rope_lane_tables(lat_len)

    def step(x_all, p):
        x_new, ckv, k_r = _layer(x_all, p, p['cache_ckv'], p['cache_krope'], dfts, rope, n_ctx, ctx_len)
        return x_new, (ckv, k_r)

    x_all = jnp.concatenate([x_prompt.reshape(1, lat_len, d), x_sample], axis=0)
    x_all, (ckv, k_r) = lax.scan(step, x_all, layers)
    y_prompt = x_all[0].reshape(n_ctx, ctx_len, d)
    y_sample = x_all[1:]
    return y_prompt, y_sample, jnp.swapaxes(ckv, 0, 1), jnp.swapaxes(k_r, 0, 1)
```

```python
import functools
import math

import jax
import jax.numpy as jnp
from jax import lax
from jax.experimental import pallas as pl
from jax.experimental.pallas import tpu as pltpu

F32 = jnp.float32
BF16 = jnp.bfloat16

D_MODEL = 2048
DEPTH = 4
GRID_W = 64
MLA_HEADS = 8
QK_NOPE = 128
QK_ROPE = 64
QK_DIM = QK_NOPE + QK_ROPE
V_DIM = 128
Q_RANK = D_MODEL // 4
KV_RANK = D_MODEL // 8
ROPE_PAIRS = QK_ROPE // 4
ROPE_THETA = 10000.0
SC_W = D_MODEL // 4
HY_W = D_MODEL // 4
HY_ORDER = 2
HY_EMB = 33
HY_HIDDEN = 64
PEER_HEADS = 8
N_KEYS = 128
N_EXPERTS = N_KEYS * N_KEYS
PK_HALF = 128
PEER_TOPK = 16
N_MOD = 6
ALPHA = (2 * DEPTH) ** 0.25
SM_SCALE = QK_DIM ** -0.5
LN_EPS = 1e-5
RMS_EPS = 1e-6

KR_OFF = Q_RANK + KV_RANK
SC_OFF = 1024
HY_OFF = SC_OFF + 3 * SC_W
IN_COLS_PAD = HY_OFF + 3 * HY_W

VMEM_LIMIT = 52 * 1024 * 1024
LANES = 128
SUBLANES = 8


def _cparams(sem):
    return pltpu.CompilerParams(dimension_semantics=sem, vmem_limit_bytes=VMEM_LIMIT)


def _mm_kernel(a_ref, b_ref, o_ref, acc_ref, *, nk):
    part = jnp.dot(a_ref[...].astype(BF16), b_ref[...].astype(BF16), preferred_element_type=F32)
    if nk == 1:
        o_ref[...] = part.astype(o_ref.dtype)
        return
    k = pl.program_id(3)

    @pl.when(k == 0)
    def _():
        acc_ref[...] = part

    @pl.when(k > 0)
    def _():
        acc_ref[...] += part

    @pl.when(k == nk - 1)
    def _():
        o_ref[...] = acc_ref[...].astype(o_ref.dtype)


def _pick(n, pref):
    t = min(n, pref)
    while n % t:
        t //= 2
    return t


def matmul(a, b, *, tm=512, tn=512, tk=2048, out_dtype=F32):
    squeeze = a.ndim == 2 and b.ndim == 2
    if a.ndim == 2:
        a = a[None]
    if b.ndim == 2:
        b = b[None]
    ba, m, k = a.shape
    bb, k2, n = b.shape
    assert k == k2 and (ba == bb or 1 in (ba, bb))
    nb = max(ba, bb)
    tm, tn, tk = _pick(m, tm), _pick(n, tn), _pick(k, tk)
    nk = k // tk
    out = pl.pallas_call(
        functools.partial(_mm_kernel, nk=nk),
        out_shape=jax.ShapeDtypeStruct((nb, m, n), out_dtype),
        grid=(nb, m // tm, n // tn, nk),
        in_specs=[
            pl.BlockSpec((None, tm, tk), lambda g, i, j, kk: (g if ba > 1 else 0, i, kk)),
            pl.BlockSpec((None, tk, tn), lambda g, i, j, kk: (g if bb > 1 else 0, kk, j)),
        ],
        out_specs=pl.BlockSpec((None, tm, tn), lambda g, i, j, kk: (g, i, j)),
        scratch_shapes=[pltpu.VMEM((tm, tn), F32)],
        compiler_params=_cparams(("parallel", "parallel", "parallel", "arbitrary")),
    )(a, b)
    return out[0] if squeeze else out


def _mm_f32_kernel(a_ref, b_ref, o_ref):
    o_ref[...] = jnp.dot(a_ref[...], b_ref[...], preferred_element_type=F32,
                         precision=lax.Precision.HIGHEST)


def matmul_f32(a, b, *, tm=512):
    m, k = a.shape
    n = b.shape[1]
    tm = _pick(m, tm)
    return pl.pallas_call(
        _mm_f32_kernel,
        out_shape=jax.ShapeDtypeStruct((m, n), F32),
        grid=(m // tm,),
        in_specs=[pl.BlockSpec((tm, k), lambda i: (i, 0)), pl.BlockSpec((k, n), lambda i: (0, 0))],
        out_specs=pl.BlockSpec((tm, n), lambda i: (i, 0)),
        compiler_params=_cparams(("parallel",)),
    )(a, b)


def _inproj_kernel(x_ref, sc_ref, sh_ref, w_ref, o_ref):
    h = x_ref[...] * (1.0 + sc_ref[...]) + sh_ref[...]
    o_ref[...] = jnp.dot(h.astype(BF16), w_ref[...], preferred_element_type=F32)


def in_projection(x, scale, shift, w, *, tm=512, tn=2048):
    g, l, d = x.shape
    n = w.shape[1]
    tm, tn = _pick(l, tm), _pick(n, tn)
    return pl.pallas_call(
        _inproj_kernel,
        out_shape=jax.ShapeDtypeStruct((g, l, n), F32),
        grid=(g, l // tm, n // tn),
        in_specs=[
            pl.BlockSpec((None, tm, d), lambda b, i, j: (b, i, 0)),
            pl.BlockSpec((None, 1, d), lambda b, i, j: (b, 0, 0)),
            pl.BlockSpec((None, 1, d), lambda b, i, j: (b, 0, 0)),
            pl.BlockSpec((d, tn), lambda b, i, j: (0, j)),
        ],
        out_specs=pl.BlockSpec((None, tm, tn), lambda b, i, j: (b, i, j)),
        compiler_params=_cparams(("parallel", "parallel", "arbitrary")),
    )(x, scale, shift, w)


def _ln_rows(y, g, b):
    mu = jnp.mean(y, axis=-1, keepdims=True)
    yc = y - mu
    var = jnp.mean(yc * yc, axis=-1, keepdims=True)
    return yc * lax.rsqrt(var + LN_EPS) * g + b


def _outproj_kernel(cat_ref, x_ref, gate_ref, w_ref, g_ref, b_ref, o_ref):
    mix = jnp.dot(cat_ref[...].astype(BF16), w_ref[...], preferred_element_type=F32)
    o_ref[...] = _ln_rows(ALPHA * x_ref[...] + gate_ref[...] * mix, g_ref[...], b_ref[...])


def out_projection(cat, x, gate, w, ln_g, ln_b, *, tm=512):
    g, l, d = x.shape
    tm = _pick(l, tm)
    tok = pl.BlockSpec((None, tm, d), lambda b, i: (b, i, 0))
    vec = pl.BlockSpec((1, d), lambda b, i: (0, 0))
    return pl.pallas_call(
        _outproj_kernel,
        out_shape=jax.ShapeDtypeStruct((g, l, d), F32),
        grid=(g, l // tm),
        in_specs=[tok, tok, pl.BlockSpec((None, 1, d), lambda b, i: (b, 0, 0)),
                  pl.BlockSpec((d, d), lambda b, i: (0, 0)), vec, vec],
        out_specs=tok,
        compiler_params=_cparams(("parallel", "parallel")),
    )(cat, x, gate, w, ln_g.reshape(1, d), ln_b.reshape(1, d))


def _conv3_rows(u, w):
    l = u.shape[0]
    row = lax.broadcasted_iota(jnp.int32, u.shape, 0)
    prev = jnp.where(row == 0, 0.0, pltpu.roll(u, 1, 0))
    nxt = jnp.where(row == l - 1, 0.0, pltpu.roll(u, l - 1, 0))
    return prev * w[0:1] + u * w[1:2] + nxt * w[2:3]


def _gated_conv_kernel(bg_ref, cg_ref, hs_ref, w_ref, o_ref):
    o_ref[...] = bg_ref[...] * _conv3_rows(cg_ref[...] * hs_ref[...], w_ref[...])


def _bias_conv_kernel(x_ref, w_ref, b_ref, o_ref):
    o_ref[...] = _conv3_rows(x_ref[...], w_ref[...]) + b_ref[...]


def _col_spec(l, batch0, col0):
    return pl.BlockSpec((None, l, LANES), lambda b, j: (b + batch0, 0, col0 + j))


def short_gated_conv(proj, w, nb, batch0):
    l = proj.shape[1]
    c0 = SC_OFF // LANES
    nj = SC_W // LANES
    return pl.pallas_call(
        _gated_conv_kernel,
        out_shape=jax.ShapeDtypeStruct((nb, l, SC_W), F32),
        grid=(nb, nj),
        in_specs=[_col_spec(l, batch0, c0), _col_spec(l, batch0, c0 + nj), _col_spec(l, batch0, c0 + 2 * nj),
                  pl.BlockSpec((3, LANES), lambda b, j: (0, j))],
        out_specs=pl.BlockSpec((None, l, LANES), lambda b, j: (b, 0, j)),
        compiler_params=_cparams(("parallel", "parallel")),
    )(proj, proj, proj, w)


def hyena_input_conv(proj, w, bias, nb, batch0):
    l = proj.shape[1]
    width = 3 * HY_W
    return pl.pallas_call(
        _bias_conv_kernel,
        out_shape=jax.ShapeDtypeStruct((nb, l, width), F32),
        grid=(nb, width // LANES),
        in_specs=[_col_spec(l, batch0, HY_OFF // LANES),
                  pl.BlockSpec((3, LANES), lambda b, j: (0, j)),
                  pl.BlockSpec((1, LANES), lambda b, j: (0, j))],
        out_specs=pl.BlockSpec((None, l, LANES), lambda b, j: (b, 0, j)),
        compiler_params=_cparams(("parallel", "parallel")),
    )(proj, w, bias.reshape(1, width))


NOPE_W = MLA_HEADS * QK_NOPE
ROPE_W = MLA_HEADS * QK_ROPE


def _qproj_kernel(p_ref, g_ref, w_ref, c_ref, s_ref, o_ref):
    x = p_ref[...]
    xn = x * lax.rsqrt(jnp.mean(x * x, axis=-1, keepdims=True) + RMS_EPS) * g_ref[...]
    acc = jnp.dot(xn.astype(BF16), w_ref[...], preferred_element_type=F32)
    rope = acc[:, NOPE_W:NOPE_W + ROPE_W] * c_ref[...] + acc[:, NOPE_W + ROPE_W:] * s_ref[...]
    for h in range(MLA_HEADS):
        o_ref[h, :, 0:QK_NOPE] = (acc[:, h * QK_NOPE:(h + 1) * QK_NOPE] * SM_SCALE).astype(BF16)
        o_ref[h, :, QK_NOPE:QK_DIM] = (rope[:, h * QK_ROPE:(h + 1) * QK_ROPE] * SM_SCALE).astype(BF16)


def q_projection(proj, g_norm, w, rope_c, rope_s, *, tm=512):
    g, l, _ = proj.shape
    tm = _pick(l, tm)
    tab = pl.BlockSpec((None, tm, ROPE_W), lambda b, i: (jnp.minimum(b, 1), i, 0))
    return pl.pallas_call(
        _qproj_kernel,
        out_shape=jax.ShapeDtypeStruct((g, MLA_HEADS, l, QK_DIM), BF16),
        grid=(g, l // tm),
        in_specs=[
            pl.BlockSpec((None, tm, Q_RANK), lambda b, i: (b, i, 0)),
            pl.BlockSpec((1, Q_RANK), lambda b, i: (0, 0)),
            pl.BlockSpec(w.shape, lambda b, i: (0, 0)),
            tab, tab,
        ],
        out_specs=pl.BlockSpec((None, MLA_HEADS, tm, QK_DIM), lambda b, i: (b, 0, i, 0)),
        compiler_params=_cparams(("parallel", "parallel")),
    )(proj, g_norm.reshape(1, Q_RANK), w, rope_c, rope_s)


def _kvproj_kernel(c_ref, kr_ref, w_ref, k_ref, v_ref):
    kv = jnp.dot(c_ref[...].astype(BF16), w_ref[...], preferred_element_type=F32)
    kr = kr_ref[...].astype(BF16)
    width = QK_NOPE + V_DIM
    for h in range(MLA_HEADS):
        k_ref[h, :, 0:QK_NOPE] = kv[:, h * width:h * width + QK_NOPE].astype(BF16)
        k_ref[h, :, QK_NOPE:QK_DIM] = kr
        v_ref[h] = kv[:, h * width + QK_NOPE:(h + 1) * width].astype(BF16)


def kv_projection(ckv, kr, w, *, tm=256):
    b, lk, _ = ckv.shape
    tm = _pick(lk, tm)
    return pl.pallas_call(
        _kvproj_kernel,
        out_shape=(jax.ShapeDtypeStruct((b, MLA_HEADS, lk, QK_DIM), BF16),
                   jax.ShapeDtypeStruct((b, MLA_HEADS, lk, V_DIM), BF16)),
        grid=(b, lk // tm),
        in_specs=[
            pl.BlockSpec((None, tm, KV_RANK), lambda bi, i: (bi, i, 0)),
            pl.BlockSpec((None, tm, QK_ROPE), lambda bi, i: (bi, i, 0)),
            pl.BlockSpec(w.shape, lambda bi, i: (0, 0)),
        ],
        out_specs=(pl.BlockSpec((None, MLA_HEADS, tm, QK_DIM), lambda bi, i: (bi, 0, i, 0)),
                   pl.BlockSpec((None, MLA_HEADS, tm, V_DIM), lambda bi, i: (bi, 0, i, 0))),
        compiler_params=_cparams(("parallel", "parallel")),
    )(ckv, kr, w)


def _attn_kernel(q_ref, k_ref, v_ref, o_ref):
    s = lax.dot_general(q_ref[...], k_ref[...], (((1,), (1,)), ((), ())),
                        preferred_element_type=F32)
    m = jnp.max(s, axis=-1, keepdims=True)
    p = jnp.exp(s - m)
    l = jnp.sum(p, axis=-1, keepdims=True)
    o = jnp.dot(p.astype(BF16), v_ref[...], preferred_element_type=F32)
    o_ref[...] = o / l


def attention(q, k, v, *, q_batch0=0, tq=256):
    b, h, lk, dq = k.shape
    l = q.shape[2]
    tq = _pick(l, tq)
    return pl.pallas_call(
        _attn_kernel,
        out_shape=jax.ShapeDtypeStruct((b, l, h * V_DIM), F32),
        grid=(b, h, l // tq),
        in_specs=[
            pl.BlockSpec((None, None, tq, dq), lambda bi, hi, i: (bi + q_batch0, hi, i, 0)),
            pl.BlockSpec((None, None, lk, dq), lambda bi, hi, i: (bi, hi, 0, 0)),
            pl.BlockSpec((None, None, lk, V_DIM), lambda bi, hi, i: (bi, hi, 0, 0)),
        ],
        out_specs=pl.BlockSpec((None, tq, V_DIM), lambda bi, hi, i: (bi, i, hi)),
        compiler_params=_cparams(("parallel", "parallel", "arbitrary")),
    )(q, k, v)


NEG_INF = float("-inf")


def _exchange(xs, i, j):
    xs[i], xs[j] = jnp.maximum(xs[i], xs[j]), jnp.minimum(xs[i], xs[j])


def _bitonic_merge_desc(xs):
    n = len(xs)
    j = n // 2
    while j >= 1:
        for i in range(n):
            if i & j == 0:
                _exchange(xs, i, i | j)
        j //= 2


def _sort_desc(xs):
    n = len(xs)
    k = 2
    while k <= n:
        j = k // 2
        while j >= 1:
            for i in range(n):
                l = i ^ j
                if l > i:
                    if i & k == 0:
                        _exchange(xs, i, l)
                    else:
                        _exchange(xs, l, i)
            j //= 2
        k *= 2


def _all_sublanes_max(m):
    for shift in (4, 2, 1):
        m = jnp.maximum(m, pltpu.roll(m, shift, 0))
    return m


def _top16_sorted(s):
    xs = [s[v * SUBLANES:(v + 1) * SUBLANES] for v in range(N_KEYS // SUBLANES)]
    _sort_desc(xs)
    for shift in (4, 2, 1):
        xs = [jnp.maximum(xs[r], pltpu.roll(xs[PEER_TOPK - 1 - r], shift, 0)) for r in range(PEER_TOPK)]
        _bitonic_merge_desc(xs)
    return xs


def _spread(rows, sub):
    out = rows[0]
    for b in range(1, SUBLANES):
        out = jnp.where(sub == b, rows[b], out)
    return out


def _all_sublanes_min(m):
    for shift in (4, 2, 1):
        m = jnp.minimum(m, pltpu.roll(m, shift, 0))
    return m


def _best_sums(cands, sub, count):
    ids = [sub + g * SUBLANES for g in range(len(cands))]
    far = len(cands) * SUBLANES
    best = []
    for _ in range(count):
        m = cands[0]
        for cnd in cands[1:]:
            m = jnp.maximum(m, cnd)
        m = _all_sublanes_max(m)
        best.append(m)
        first = jnp.where(cands[0] == m, ids[0], far)
        for cnd, idg in zip(cands[1:], ids[1:]):
            first = jnp.minimum(first, jnp.where(cnd == m, idg, far))
        first = _all_sublanes_min(first)
        cands = [jnp.where(idg == first, NEG_INF, cnd) for cnd, idg in zip(cands, ids)]
    return best


def _route_kernel(x_ref, sc_ref, sh_ref, wqt_ref, pk_ref, hb_ref, e1_ref, tau_ref, e2_ref, qt_ref, s_ref):
    tm = x_ref.shape[0]
    h = x_ref[...] * (1.0 + sc_ref[...]) + sh_ref[...]
    hb = h.astype(BF16)
    hb_ref[...] = hb
    qt = lax.dot_general(wqt_ref[...], hb, (((1,), (1,)), ((), ())), preferred_element_type=F32)
    for hp in range(2 * PEER_HEADS):
        qt_ref[hp] = qt[hp * PK_HALF:(hp + 1) * PK_HALF].astype(BF16)

    def head(hd, carry):
        for p in range(2):
            s_ref[p] = jnp.dot(pk_ref[2 * hd + p], qt_ref[2 * hd + p], preferred_element_type=F32)
        for lt in range(tm // LANES):
            ls = slice(lt * LANES, (lt + 1) * LANES)
            s1 = s_ref[0, :, ls]
            s2 = s_ref[1, :, ls]
            v1 = _top16_sorted(s1)
            v2 = _top16_sorted(s2)
            sub = lax.broadcasted_iota(jnp.int32, (SUBLANES, LANES), 0)
            v2_lo, v2_hi = _spread(v2[:SUBLANES], sub), _spread(v2[SUBLANES:], sub)
            v1_hi = _spread(v1[SUBLANES:], sub)
            cands = [v1[0] + v2_lo, v1[0] + v2_hi, v1_hi + v2[0]]
            cands += [v1[a] + v2_lo for a in range(1, SUBLANES)]
            best = _best_sums(cands, sub, PEER_TOPK + 1)
            z = jnp.ones_like(best[0])
            for k in range(1, PEER_TOPK):
                z = z + jnp.exp(best[k] - best[0])
            thr = 0.5 * (best[PEER_TOPK - 1] + best[PEER_TOPK])
            m1 = v1[0][0:1]
            m2 = v2[0][0:1]
            in1 = s1 >= v1[PEER_TOPK - 1][0:1]
            in2 = s2 >= v2[PEER_TOPK - 1][0:1]
            e1_ref[hd, :, ls] = jnp.exp(s1 - m1) / z[0:1]
            e2_ref[hd, :, ls] = jnp.where(in2, jnp.exp(s2 - m2), 0.0)
            tau_ref[hd, :, ls] = jnp.where(in1, jnp.exp((thr[0:1] - m2) - s1), jnp.inf)
        return carry

    lax.fori_loop(0, PEER_HEADS, head, 0)


def peer_route(x, scale, shift, wqt, pkeys, *, tm=256):
    g, l, d = x.shape
    t = g * l
    tm = _pick(l, tm)
    nl = l // tm
    fac = jax.ShapeDtypeStruct((PEER_HEADS, N_KEYS, t), F32)
    fspec = pl.BlockSpec((PEER_HEADS, N_KEYS, tm), lambda b, i: (0, 0, b * nl + i))
    return pl.pallas_call(
        _route_kernel,
        out_shape=(jax.ShapeDtypeStruct((g, l, d), BF16), fac, fac, fac),
        grid=(g, nl),
        in_specs=[
            pl.BlockSpec((None, tm, d), lambda b, i: (b, i, 0)),
            pl.BlockSpec((None, 1, d), lambda b, i: (b, 0, 0)),
            pl.BlockSpec((None, 1, d), lambda b, i: (b, 0, 0)),
            pl.BlockSpec(wqt.shape, lambda b, i: (0, 0)),
            pl.BlockSpec(pkeys.shape, lambda b, i: (0, 0, 0)),
        ],
        out_specs=(pl.BlockSpec((None, tm, d), lambda b, i: (b, i, 0)), fspec, fspec, fspec),
        scratch_shapes=[
            pltpu.VMEM((2 * PEER_HEADS, PK_HALF, tm), BF16),
            pltpu.VMEM((2, N_KEYS, tm), F32),
        ],
        compiler_params=_cparams(("parallel", "arbitrary")),
    )(x, scale, shift, wqt, pkeys)


GELU_C = math.sqrt(2.0 / math.pi)


def _gelu_tanh(a):
    half = 0.5 * a
    return half + half * jnp.tanh(a * (GELU_C + (GELU_C * 0.044715) * (a * a)))


EXPERT_CHUNK = SUBLANES * N_KEYS


def _experts_kernel(hb_ref, e1_ref, tau_ref, e2_ref, u_ref, vt_ref, x_ref, gate_ref, g_ref, b_ref,
                    o_ref, acc_ref, act_ref, p_ref, *, nc):
    c = pl.program_id(2)
    tm = hb_ref.shape[0]

    @pl.when(c == 0)
    def _():
        acc_ref[...] = jnp.zeros_like(acc_ref)

    act_ref[...] = lax.dot_general(u_ref[...], hb_ref[...], (((1,), (1,)), ((), ())),
                                   preferred_element_type=F32)
    for lt in range(tm // LANES):
        ls = slice(lt * LANES, (lt + 1) * LANES)
        for j in range(SUBLANES):
            w = jnp.zeros((N_KEYS, LANES), F32)
            for hd in range(PEER_HEADS):
                e2 = e2_ref[hd, :, ls]
                w = w + jnp.where(e2 >= tau_ref[hd, j:j + 1, ls], e2, 0.0) * e1_ref[hd, j:j + 1, ls]
            rs = slice(j * N_KEYS, (j + 1) * N_KEYS)
            p_ref[rs, ls] = (w * _gelu_tanh(act_ref[rs, ls])).astype(BF16)
    acc_ref[...] += jnp.dot(vt_ref[...], p_ref[...], preferred_element_type=F32)

    @pl.when(c == nc - 1)
    def _():
        y = ALPHA * x_ref[...] + gate_ref[...] * acc_ref[...].T
        o_ref[...] = _ln_rows(y, g_ref[...], b_ref[...])


def peer_experts(hb, e1, tau, e2, u, vt, x, gate, ln_g, ln_b, *, tm=512):
    g, l, d = x.shape
    tm = _pick(l, tm)
    nl = l // tm
    ec = EXPERT_CHUNK
    nc = N_EXPERTS // ec
    rspec = pl.BlockSpec((PEER_HEADS, SUBLANES, tm), lambda b, i, c: (0, c, b * nl + i))
    fspec = pl.BlockSpec((PEER_HEADS, N_KEYS, tm), lambda b, i, c: (0, 0, b * nl + i))
    tok = pl.BlockSpec((None, tm, d), lambda b, i, c: (b, i, 0))
    tok_res = pl.BlockSpec((None, tm, d), lambda b, i, c: (b, i, 0), pipeline_mode=pl.Buffered(1))
    vec = pl.BlockSpec((1, d), lambda b, i, c: (0, 0))
    return pl.pallas_call(
        functools.partial(_experts_kernel, nc=nc),
        out_shape=jax.ShapeDtypeStruct((g, l, d), F32),
        grid=(g, nl, nc),
        in_specs=[
            tok, rspec, rspec, fspec,
            pl.BlockSpec((ec, d), lambda b, i, c: (c, 0)),
            pl.BlockSpec((d, ec), lambda b, i, c: (0, c)),
            tok_res, pl.BlockSpec((None, 1, d), lambda b, i, c: (b, 0, 0)), vec, vec,
        ],
        out_specs=tok,
        scratch_shapes=[
            pltpu.VMEM((d, tm), F32),
            pltpu.VMEM((ec, tm), F32),
            pltpu.VMEM((ec, tm), BF16),
        ],
        compiler_params=_cparams(("parallel", "parallel", "arbitrary")),
    )(hb, e1, tau, e2, u, vt, x, gate, ln_g.reshape(1, d), ln_b.reshape(1, d))


def _rms_norm(x, g):
    return x * lax.rsqrt(jnp.mean(jnp.square(x), axis=-1, keepdims=True) + RMS_EPS) * g


def _rope_tables(l):
    rows = l // GRID_W
    row = jnp.repeat(jnp.arange(rows), GRID_W)
    col = jnp.tile(jnp.arange(GRID_W), rows)
    pos = jnp.stack([row, col], axis=-1).astype(F32)
    inv = ROPE_THETA ** (-jnp.arange(ROPE_PAIRS, dtype=F32) / ROPE_PAIRS)
    ang = pos[:, :, None] * inv
    return jnp.cos(ang), jnp.sin(ang)


def _rope_lane_tables(l):
    cos, sin = _rope_tables(l)
    c = jnp.broadcast_to(cos[:, :, None, :], (l, 2, 2, ROPE_PAIRS)).reshape(l, QK_ROPE)
    s = jnp.stack([-sin, sin], axis=2).reshape(l, QK_ROPE)
    c = jnp.tile(c, (1, MLA_HEADS))
    s = jnp.tile(s, (1, MLA_HEADS))
    return jnp.stack([jnp.ones_like(c), c]), jnp.stack([jnp.zeros_like(s), s])


def _extend_w_uq(w_uq):
    depth = w_uq.shape[0]
    w4 = w_uq.reshape(depth, Q_RANK, MLA_HEADS, QK_DIM)
    nope = w4[..., :QK_NOPE].reshape(depth, Q_RANK, NOPE_W)
    rope = w4[..., QK_NOPE:]
    swapped = jnp.flip(rope.reshape(depth, Q_RANK, MLA_HEADS, 2, 2, ROPE_PAIRS), axis=-2)
    return jnp.concatenate([nope, rope.reshape(depth, Q_RANK, ROPE_W),
                            swapped.reshape(depth, Q_RANK, ROPE_W)], axis=-1).astype(BF16)


def _apply_rope(x, cos, sin):
    b, l, h, _ = x.shape
    xr = x.reshape(b, l, h, 2, 2, ROPE_PAIRS)
    x1, x2 = xr[..., 0, :], xr[..., 1, :]
    c = cos[None, :, None]
    s = sin[None, :, None]
    out = jnp.stack([x1 * c - x2 * s, x2 * c + x1 * s], axis=-2)
    return out.reshape(b, l, h, QK_ROPE)


def _dft_matrices(l):
    n = 2 * l
    k = jnp.arange(l, dtype=jnp.int32)[:, None]
    t = jnp.arange(n, dtype=jnp.int32)[None, :]
    step = 64
    t_hi = jnp.arange(n // step, dtype=jnp.int32)[None, :] * step
    t_lo = jnp.arange(step, dtype=jnp.int32)[None, :]
    a_hi = ((k * t_hi) % n).astype(F32) * (2.0 * math.pi / n)
    a_lo = ((k * t_lo) % n).astype(F32) * (2.0 * math.pi / n)
    ch, sh = jnp.cos(a_hi)[:, :, None], jnp.sin(a_hi)[:, :, None]
    cl, sl = jnp.cos(a_lo)[:, None, :], jnp.sin(a_lo)[:, None, :]
    cos = (ch * cl - sh * sl).reshape(l, n)
    msin = -(sh * cl + ch * sl).reshape(l, n)
    nyq = jnp.where(t % 2 == 0, 1.0, -1.0).astype(F32)
    msin = jnp.where(k == 0, nyq, msin)
    fwd = jnp.concatenate([cos, msin], axis=0)
    wgt = jnp.where(k == 0, 1.0 / n, 2.0 / n)
    inv = jnp.concatenate([cos[:, :l] * wgt, msin[:, :l] * wgt], axis=0).T
    fwd = fwd.astype(BF16)
    return fwd, fwd[:, :l], inv.astype(BF16)


def _hyena_filter_spectrum(l, p, fwd):
    t = jnp.linspace(0.0, 1.0, l, dtype=F32)[:, None]
    bands = (HY_EMB - 1) // 2
    w = 2.0 * math.pi * jnp.arange(l, dtype=F32) / l
    fr_b = jnp.linspace(1e-4, bands - 1, bands, dtype=F32)
    ang = w[:, None] * fr_b[None]
    z = jnp.concatenate([t, jnp.cos(ang), -jnp.sin(ang)], axis=-1)
    z = jnp.concatenate([z, z[::-1]], axis=0)
    t2 = jnp.concatenate([t, t[::-1]], axis=0)
    zp = jnp.pad(z, ((0, 0), (0, LANES - HY_EMB)))
    w1 = jnp.pad(p['hy_f_w1'], ((0, LANES - HY_EMB), (0, LANES - HY_HIDDEN)))
    w2 = jnp.pad(p['hy_f_w2'], ((0, LANES - HY_HIDDEN), (0, LANES - HY_HIDDEN)))
    w3 = jnp.pad(p['hy_f_w3'], ((0, LANES - HY_HIDDEN), (0, 0)))
    pad_h = (0, LANES - HY_HIDDEN)
    fr = jnp.pad(p['hy_f_freq'], pad_h)
    hid = jnp.sin(fr * (matmul_f32(zp, w1) + jnp.pad(p['hy_f_b1'], pad_h)))
    hid = jnp.sin(fr * (matmul_f32(hid, w2) + jnp.pad(p['hy_f_b2'], pad_h)))
    half = HY_ORDER * HY_W
    decay = jnp.abs(p['hy_decay']).reshape(2, half)
    h_fwd = matmul_f32(hid[:l], w3[:, :half]) * jnp.exp(-t2[:l] * decay[0])
    h_bwd = matmul_f32(hid[l:], w3[:, half:]) * jnp.exp(-t2[l:] * decay[1])
    k = jnp.concatenate([h_fwd, jnp.zeros((1, half), F32), h_bwd[:l - 1]], axis=0)
    k = k.reshape(2 * l, HY_ORDER, HY_W)
    k = k / jnp.sum(jnp.abs(k), axis=0, keepdims=True)
    spec = matmul(fwd, k.reshape(2 * l, HY_ORDER * HY_W))
    kr, ki = spec[:l], spec[l:]
    first = (jnp.arange(l) == 0)[:, None]
    a = kr
    b = jnp.where(first, 0.0, ki)
    d = jnp.where(first, ki, kr)
    return a, b, d


def _long_conv(z, coef, fwd_half, inv, o):
    l = z.shape[1]
    a, b, d = (c[:, o * HY_W:(o + 1) * HY_W] for c in coef)
    spec = matmul(fwd_half, z.astype(BF16), tm=2048)
    xr, xi = spec[:, :l], spec[:, l:]
    y = jnp.concatenate([xr * a - xi * b, xr * b + xi * d], axis=1).astype(BF16)
    return matmul(inv, y, tm=2048)


def _hyena(u, p, dft, coef):
    _, fwd_half, inv = dft
    v, x1, x2 = jnp.split(u, 3, axis=-1)
    z = v
    for o, gate in enumerate((x1, x2)):
        z = gate * (_long_conv(z, coef, fwd_half, inv, o) + p['hy_bias'][o] * z)
    return z


def _mixers(proj, qf, nb, batch0, p, dft, coef, ctx_ckv, ctx_krope, rope):
    kv_lat = proj[batch0:batch0 + nb, :, Q_RANK:KR_OFF]
    k_r = proj[batch0:batch0 + nb, :, KR_OFF:KR_OFF + QK_ROPE]
    ckv = _rms_norm(kv_lat, p['kv_norm'])
    if rope is None:
        ckv_all, kr_all = ckv, k_r
    else:
        cos, sin = rope
        kr_rot = _apply_rope(k_r[:, :, None], cos, sin)[:, :, 0]
        ckv_all = jnp.concatenate([ctx_ckv, ckv], axis=1)
        kr_all = jnp.concatenate([ctx_krope, kr_rot], axis=1)
    kf, vf = kv_projection(ckv_all, kr_all, p['w_ukv'])
    att = attention(qf, kf, vf, q_batch0=batch0)
    sc = short_gated_conv(proj, p['sc_conv'], nb, batch0)
    hy = _hyena(hyena_input_conv(proj, p['hy_sconv_w'], p['hy_sconv_b'], nb, batch0), p, dft, coef)
    return jnp.concatenate([att, sc, hy], axis=-1), ckv, k_r


def _layer(x_all, p, cache_ckv_l, cache_krope_l, dfts, rope, n_ctx, ctx_len):
    g, lg, d = x_all.shape
    mod = p['mod']
    proj = in_projection(x_all, mod[:, 1:2], mod[:, 0:1], p['w_in'])
    coef_c = _hyena_filter_spectrum(ctx_len, p, dfts[0][0])
    coef_s = _hyena_filter_spectrum(lg, p, dfts[1][0])
    qf = q_projection(proj, p['q_norm'], p['w_uq'], rope[2], rope[3])
    q_ctx = qf[0].reshape(MLA_HEADS, n_ctx, ctx_len, QK_DIM).transpose(1, 0, 2, 3)
    cat_c, ckv, k_r = _mixers(proj[0].reshape(n_ctx, ctx_len, -1), q_ctx, n_ctx, 0, p, dfts[0], coef_c,
                              None, None, None)
    cat_s, _, _ = _mixers(proj, qf, g - 1, 1, p, dfts[1], coef_s, cache_ckv_l, cache_krope_l, rope[:2])
    cat = jnp.concatenate([cat_c.reshape(1, lg, d), cat_s], axis=0)
    x1 = out_projection(cat, x_all, mod[:, 2:3], p['w_out'], p['ln1_g'], p['ln1_b'])
    hb, e1, tau, e2 = peer_route(x1, mod[:, 4:5], mod[:, 3:4], p['wqt'], p['pkeys'])
    x2 = peer_experts(hb, e1, tau, e2, p['peer_u'], p['peer_vt'], x1, mod[:, 5:6], p['ln2_g'], p['ln2_b'])
    return x2, ckv, k_r


def kernel(x_prompt, x_sample, cache_ckv, cache_krope, c, c_ctx, w_mod, b_mod, w_in, mla_q_norm, mla_kv_norm, w_uq, w_ukv, sc_conv, hy_sconv_w, hy_sconv_b, hy_f_w1, hy_f_b1, hy_f_w2, hy_f_b2, hy_f_w3, hy_f_freq, hy_decay, hy_bias, w_out, ln1_g, ln1_b, ln2_g, ln2_b, pk_query, pk_keys, peer_u, peer_v):
    n_ctx, ctx_len, d = x_prompt.shape
    n_lat, lat_len, _ = x_sample.shape
    depth = w_in.shape[0]
    assert n_ctx * ctx_len == lat_len
    g = n_lat + 1

    cond = jnp.concatenate([c_ctx[None], c], axis=0)
    cond = jnp.pad(jax.nn.silu(cond), ((0, 16 - g), (0, 0)))
    mod = matmul(cond, w_mod, tm=16, tn=1536)[:, :g] + b_mod[:, None]
    mod = mod.reshape(depth, g, N_MOD, d)

    w_in_p = jnp.concatenate(
        [w_in[..., :KR_OFF + QK_ROPE],
         jnp.zeros((depth, d, SC_OFF - KR_OFF - QK_ROPE), w_in.dtype),
         w_in[..., KR_OFF + QK_ROPE:]], axis=-1).astype(BF16)
    layers = {
        'mod': mod, 'w_in': w_in_p, 'q_norm': mla_q_norm, 'kv_norm': mla_kv_norm,
        'w_uq': _extend_w_uq(w_uq), 'w_ukv': w_ukv.astype(BF16), 'sc_conv': sc_conv,
        'hy_sconv_w': hy_sconv_w, 'hy_sconv_b': hy_sconv_b, 'hy_f_w1': hy_f_w1, 'hy_f_b1': hy_f_b1,
        'hy_f_w2': hy_f_w2, 'hy_f_b2': hy_f_b2, 'hy_f_w3': hy_f_w3, 'hy_f_freq': hy_f_freq,
        'hy_decay': hy_decay, 'hy_bias': hy_bias, 'w_out': w_out.astype(BF16),
        'ln1_g': ln1_g, 'ln1_b': ln1_b, 'ln2_g': ln2_g, 'ln2_b': ln2_b,
        'wqt': jnp.swapaxes(pk_query, 1, 2).astype(BF16),
        'pkeys': pk_keys.reshape(depth, 2 * PEER_HEADS, N_KEYS, PK_HALF).astype(BF16),
        'peer_u': peer_u.astype(BF16),
        'peer_vt': jnp.swapaxes(peer_v.astype(BF16), 1, 2),
        'cache_ckv': jnp.swapaxes(cache_ckv, 0, 1),
        'cache_krope': jnp.swapaxes(cache_krope, 0, 1),
    }
    dfts = (_dft_matrices(ctx_len), _dft_matrices(lat_len))
    rope = _rope_tables(lat_len) + _rope_lane_tables(lat_len)

    def step(x_all, p):
        x_new, ckv, k_r = _layer(x_all, p, p['cache_ckv'], p['cache_krope'], dfts, rope, n_ctx, ctx_len)
        return x_new, (ckv, k_r)

    x_all = jnp.concatenate([x_prompt.reshape(1, lat_len, d), x_sample], axis=0)
    x_all, (ckv, k_r) = lax.scan(step, x_all, layers)
    y_prompt = x_all[0].reshape(n_ctx, ctx_len, d)
    y_sample = x_all[1:]
    return y_prompt, y_sample, jnp.swapaxes(ckv, 0, 1), jnp.swapaxes(k_r, 0, 1)
```
